```python
import math
import jax, jax.numpy as jnp
from jax import lax
import numpy as np


D_MODEL = 1024
BATCH = 16
SEQ = 256
DEPTH = 4
DEC_BATCH = 8
DEC_SEQ = 4096
PAST_LEN = 512

GRID_W = 64
CHUNK = 128
Q_BLOCK = 128
HEAD_DIM = 64
D_A = D_MODEL // 2
SG_GROUP_W = 128
G_A = D_A // SG_GROUP_W
H_B = D_MODEL // 256
D_B = H_B * 2 * HEAD_DIM
H_C = D_MODEL // 128
D_C = H_C * HEAD_DIM
WIN_R = 8
WIN_W = 16
ROPE_THETA = 10000.0
ALPHA = (2 * DEPTH) ** 0.25
BETA = (8 * DEPTH) ** -0.25
EPS = 1e-6
NEG_INF = -1e30
SPLIT_SIZES = (D_A,) * 3 + (D_B,) * 4 + (D_C,) * 4
SPLIT_POINTS = tuple(int(s) for s in np.cumsum(SPLIT_SIZES)[:-1])
D_IN = int(sum(SPLIT_SIZES))

kernel_name = "hybrid_diffusion_gmlp_diffattn_natten_step"


def layernorm(x):
    xf = x.astype(jnp.float32)
    mu = jnp.mean(xf, axis=-1, keepdims=True)
    var = jnp.mean(jnp.square(xf - mu), axis=-1, keepdims=True)
    return ((xf - mu) * lax.rsqrt(var + EPS)).astype(x.dtype)


def modulate(x, mod):
    shift, scale, gate = jnp.split(mod, 3, axis=-1)
    return layernorm(x) * (1 + scale) + shift, gate


def post_norm(x, out, gate, g, b):
    return layernorm(ALPHA * x + gate * out) * g + b


def axial_rope_tables(n_tok):
    t = jnp.arange(n_tok)
    half = HEAD_DIM // 2
    inv = 1.0 / (ROPE_THETA ** (jnp.arange(0, half, 2, dtype=jnp.float32) / half))
    ang_r = (t // GRID_W).astype(jnp.float32)[:, None] * inv
    ang_c = (t % GRID_W).astype(jnp.float32)[:, None] * inv
    ang = jnp.concatenate([ang_r, ang_c], axis=-1)
    return jnp.cos(ang), jnp.sin(ang)


def apply_axial_rope(x, cos, sin):
    L = x.shape[-2]
    quarter = HEAD_DIM // 4
    xr = x.astype(jnp.float32).reshape(*x.shape[:-1], 2, 2, quarter)
    x1, x2 = xr[..., 0, :], xr[..., 1, :]
    c = cos.reshape(L, 2, quarter)
    s = sin.reshape(L, 2, quarter)
    out = jnp.stack([x1 * c - x2 * s, x2 * c + x1 * s], axis=-2)
    return out.reshape(x.shape).astype(x.dtype)


def to_heads(z, n_heads):
    B, L, _ = z.shape
    return z.reshape(B, L, n_heads, -1).transpose(0, 2, 1, 3)


def from_heads(o):
    B, H, L, d = o.shape
    return o.transpose(0, 2, 1, 3).reshape(B, L, H * d)


def diff_heads(zq, zk, zv):
    B, L, _ = zq.shape
    q = zq.reshape(B, L, 2, H_B, HEAD_DIM).transpose(0, 2, 3, 1, 4)
    k = zk.reshape(B, L, 2, H_B, HEAD_DIM).transpose(0, 2, 3, 1, 4)
    v = zv.reshape(B, L, H_B, 2 * HEAD_DIM).transpose(0, 2, 1, 3)
    return q, k, v


def to_blocks(q):
    *lead, L, d = q.shape
    return jnp.moveaxis(q.reshape(*lead, L // Q_BLOCK, Q_BLOCK, d), -3, 0)


def from_blocks(o):
    o = jnp.moveaxis(o, 0, -3)
    return o.reshape(*o.shape[:-3], -1, o.shape[-1])


def chunk_spatial_gate(u, v, g, b, w_s, b_s):
    B, L, _ = v.shape
    vn = layernorm(v) * g + b
    vc = vn.reshape(B, L // CHUNK, CHUNK, G_A, SG_GROUP_W)
    sv = jnp.einsum("gpq,bnqgc->bnpgc", w_s, vc) + b_s.T[None, None, :, :, None]
    return u * sv.reshape(B, L, D_A)


def diff_lambda(lq1, lk1, lq2, lk2, lam_init):
    f = lambda a: a.astype(jnp.float32)
    return jnp.exp(jnp.sum(f(lq1) * f(lk1))) - jnp.exp(jnp.sum(f(lq2) * f(lk2))) + lam_init


def diff_attention(q, k, v, lam):
    scale = HEAD_DIM ** -0.5

    def block(qb):
        s = jnp.einsum("bmhqd,bmhkd->bmhqk", qb, k, preferred_element_type=jnp.float32) * scale
        a = jax.nn.softmax(s, axis=-1)
        w = a[:, 0] - lam * a[:, 1]
        return jnp.einsum("bhqk,bhkd->bhqd", w.astype(v.dtype), v)

    return from_blocks(lax.map(block, to_blocks(q)))


def diff_finish(o, g, lam_init):
    of = o.astype(jnp.float32)
    of = of * lax.rsqrt(jnp.mean(of * of, axis=-1, keepdims=True) + EPS)
    of = of * g.astype(jnp.float32) * (1.0 - lam_init)
    return from_heads(of.astype(o.dtype))


def softmax_attention(q, k, v):
    scale = HEAD_DIM ** -0.5

    def block(qb):
        s = jnp.einsum("bhqd,bhkd->bhqk", qb, k, preferred_element_type=jnp.float32) * scale
        p = jax.nn.softmax(s, axis=-1)
        return jnp.einsum("bhqk,bhkd->bhqd", p.astype(v.dtype), v)

    return from_blocks(lax.map(block, to_blocks(q)))


def neighbourhood_attention(q, k, v, kc, vc, rel_bias):
    B, H, L, d = q.shape
    rows = L // GRID_W
    wr = min(WIN_R, rows)
    scale = d ** -0.5
    qg = q.reshape(B, H, rows, GRID_W, d)
    kg = k.reshape(B, H, rows, GRID_W, d)
    vg = v.reshape(B, H, rows, GRID_W, d)
    cpos = jnp.arange(GRID_W)
    cstart = jnp.clip(cpos - WIN_W // 2, 0, GRID_W - WIN_W)
    colmask = (cpos[None, :] >= cstart[:, None]) & (cpos[None, :] < cstart[:, None] + WIN_W)
    dx_idx = jnp.clip(cpos[None, :] - cpos[:, None], -(WIN_W - 1), WIN_W - 1) + WIN_W - 1

    def row_block(r):
        rs = jnp.clip(r - wr // 2, 0, rows - wr)
        qr = lax.dynamic_index_in_dim(qg, r, axis=2, keepdims=False)
        kb = lax.dynamic_slice_in_dim(kg, rs, wr, axis=2)
        vb = lax.dynamic_slice_in_dim(vg, rs, wr, axis=2)
        dy_idx = rs + jnp.arange(wr) - r + WIN_R - 1
        bias = rel_bias[:, dy_idx[None, :, None], dx_idx[:, None, :]]
        s_loc = jnp.einsum("bhqd,bhjkd->bhqjk", qr, kb, preferred_element_type=jnp.float32) * scale + bias
        s_loc = jnp.where(colmask[:, None, :], s_loc, NEG_INF)
        s_ctx = jnp.einsum("bhqd,bhkd->bhqk", qr, kc, preferred_element_type=jnp.float32) * scale
        n_loc = wr * GRID_W
        s = jnp.concatenate([s_loc.reshape(B, H, GRID_W, n_loc), s_ctx], axis=-1)
        p = jax.nn.softmax(s, axis=-1).astype(v.dtype)
        p_loc = p[..., :n_loc].reshape(B, H, GRID_W, wr, GRID_W)
        p_ctx = p[..., n_loc:]
        return (jnp.einsum("bhqjk,bhjkd->bhqd", p_loc, vb)
                + jnp.einsum("bhqk,bhkd->bhqd", p_ctx, vc))

    out = lax.map(row_block, jnp.arange(rows))
    return jnp.moveaxis(out, 0, 2).reshape(B, H, L, d)


def merge_out(h, y_a, y_b, y_c, w_mg, b_mg, w_a, w_b, w_c, w_o):
    g_a, g_b, g_c = jnp.split(jax.nn.sigmoid(h @ w_mg + b_mg), 3, axis=-1)
    m = g_a * (y_a @ w_a) + g_b * (y_b @ w_b) + g_c * (y_c @ w_c)
    return m @ w_o


def setup_inputs(seed: int = 0) -> dict:
    key = jax.random.key(seed)
    ks = jax.random.split(key, 32)
    nrm = lambda k, shape, s: jax.random.normal(k, shape, jnp.float32) * s
    D = D_MODEL
    return {
        "x_prompt": nrm(ks[0], (BATCH, SEQ, D), 1.0),
        "x_sample": nrm(ks[1], (DEC_BATCH, DEC_SEQ, D), 1.0),
        "c": nrm(ks[2], (DEC_BATCH, D), 1.0),
        "cache_diff_k": nrm(ks[3], (DEC_BATCH, DEPTH, 2, H_B, PAST_LEN, HEAD_DIM), 1.0),
        "cache_diff_v": nrm(ks[4], (DEC_BATCH, DEPTH, H_B, PAST_LEN, 2 * HEAD_DIM), 1.0),
        "cache_na_k": nrm(ks[5], (DEC_BATCH, DEPTH, H_C, PAST_LEN, HEAD_DIM), 1.0),
        "cache_na_v": nrm(ks[6], (DEC_BATCH, DEPTH, H_C, PAST_LEN, HEAD_DIM), 1.0),
        "c_ctx": nrm(ks[7], (D,), 1.0),
        "w_ada": nrm(ks[8], (DEPTH, D, 3 * D), 0.5 * D ** -0.5),
        "b_ada": nrm(ks[9], (DEPTH, 3 * D), 0.02),
        "w_in": nrm(ks[10], (DEPTH, D, D_IN), D ** -0.5),
        "sg_norm_g": 1.0 + nrm(ks[11], (DEPTH, D_A), 0.02),
        "sg_norm_b": nrm(ks[12], (DEPTH, D_A), 0.02),
        "w_spatial": nrm(ks[13], (DEPTH, G_A, CHUNK, CHUNK), CHUNK ** -0.5),
        "b_spatial": 1.0 + nrm(ks[14], (DEPTH, G_A, CHUNK), 0.02),
        "lambda_q1": nrm(ks[15], (DEPTH, HEAD_DIM), 0.1),
        "lambda_k1": nrm(ks[16], (DEPTH, HEAD_DIM), 0.1),
        "lambda_q2": nrm(ks[17], (DEPTH, HEAD_DIM), 0.1),
        "lambda_k2": nrm(ks[18], (DEPTH, HEAD_DIM), 0.1),
        "diff_subln_g": 1.0 + nrm(ks[19], (DEPTH, 2 * HEAD_DIM), 0.02),
        "na_rel_bias": nrm(ks[20], (DEPTH, H_C, 2 * WIN_R - 1, 2 * WIN_W - 1), 0.1),
        "w_br_a": nrm(ks[21], (DEPTH, D_A, D), D_A ** -0.5),
        "w_br_b": nrm(ks[22], (DEPTH, D_B, D), D_B ** -0.5),
        "w_br_c": nrm(ks[23], (DEPTH, D_C, D), D_C ** -0.5),
        "w_mgate": nrm(ks[24], (DEPTH, D, 3 * D), D ** -0.5),
        "b_mgate": nrm(ks[25], (DEPTH, 3 * D), 0.02),
        "w_out": nrm(ks[26], (DEPTH, D, D), BETA * D ** -0.5),
        "ln_g": 1.0 + nrm(ks[27], (DEPTH, D), 0.02),
        "ln_b": nrm(ks[28], (DEPTH, D), 0.02),
    }


def reference(x_prompt, x_sample, c, cache_diff_k, cache_diff_v, cache_na_k, cache_na_v, c_ctx,
              w_ada, b_ada, w_in, sg_norm_g, sg_norm_b, w_spatial, b_spatial,
              lambda_q1, lambda_k1, lambda_q2, lambda_k2, diff_subln_g, na_rel_bias,
              w_br_a, w_br_b, w_br_c, w_mgate, b_mgate, w_out, ln_g, ln_b):
    cos, sin = axial_rope_tables(x_sample.shape[1])
    xp, xs = x_prompt, x_sample
    new_dk, new_dv, new_nk, new_nv = [], [], [], []
    for l in range(DEPTH):
        lam_init = 0.8 - 0.6 * math.exp(-0.3 * l)
        lam = diff_lambda(lambda_q1[l], lambda_k1[l], lambda_q2[l], lambda_k2[l], lam_init)
        merge_w = (w_mgate[l], b_mgate[l], w_br_a[l], w_br_b[l], w_br_c[l], w_out[l])

        h, gate = modulate(xp, jax.nn.silu(c_ctx) @ w_ada[l] + b_ada[l])
        u_a, v_a, g_a, q_b, k_b, v_b, g_b, q_c, k_c, v_c, g_c = jnp.split(h @ w_in[l], SPLIT_POINTS, axis=-1)
        y_a = chunk_spatial_gate(u_a, v_a, sg_norm_g[l], sg_norm_b[l], w_spatial[l], b_spatial[l]) * jax.nn.silu(g_a)
        q, k, v = diff_heads(q_b, k_b, v_b)
        y_b = diff_finish(diff_attention(q, k, v, lam), diff_subln_g[l], lam_init) * jax.nn.silu(g_b)
        qc, kc, vc = to_heads(q_c, H_C), to_heads(k_c, H_C), to_heads(v_c, H_C)
        y_c = from_heads(softmax_attention(qc, kc, vc)) * jax.nn.silu(g_c)
        out = merge_out(h, y_a, y_b, y_c, *merge_w)
        xp = post_norm(xp, out, gate, ln_g[l], ln_b[l])
        new_dk.append(k)
        new_dv.append(v)
        new_nk.append(kc)
        new_nv.append(vc)

        h, gate = modulate(xs, (jax.nn.silu(c) @ w_ada[l] + b_ada[l])[:, None, :])
        u_a, v_a, g_a, q_b, k_b, v_b, g_b, q_c, k_c, v_c, g_c = jnp.split(h @ w_in[l], SPLIT_POINTS, axis=-1)
        y_a = chunk_spatial_gate(u_a, v_a, sg_norm_g[l], sg_norm_b[l], w_spatial[l], b_spatial[l]) * jax.nn.silu(g_a)
        q, k, v = diff_heads(q_b, k_b, v_b)
        q = apply_axial_rope(q, cos, sin)
        k = apply_axial_rope(k, cos, sin)
        k_all = jnp.concatenate([k, cache_diff_k[:, l].astype(k.dtype)], axis=3)
        v_all = jnp.concatenate([v, cache_diff_v[:, l].astype(v.dtype)], axis=2)
        y_b = diff_finish(diff_attention(q, k_all, v_all, lam), diff_subln_g[l], lam_init) * jax.nn.silu(g_b)
        qc, kc, vc = to_heads(q_c, H_C), to_heads(k_c, H_C), to_heads(v_c, H_C)
        y_c = from_heads(neighbourhood_attention(qc, kc, vc, cache_na_k[:, l].astype(kc.dtype),
                                                 cache_na_v[:, l].astype(vc.dtype), na_rel_bias[l])) * jax.nn.silu(g_c)
        out = merge_out(h, y_a, y_b, y_c, *merge_w)
        xs = post_norm(xs, out, gate, ln_g[l], ln_b[l])

    new_diff_k = jnp.stack(new_dk, axis=1)
    new_diff_v = jnp.stack(new_dv, axis=1)
    new_na_k = jnp.stack(new_nk, axis=1)
    new_na_v = jnp.stack(new_nv, axis=1)
    return (xp, xs, new_diff_k, new_diff_v, new_na_k, new_na_v)
```

```python
import functools
import math

import jax
import jax.numpy as jnp
import numpy as np
from jax import lax
from jax.experimental import pallas as pl
from jax.experimental.pallas import tpu as pltpu

GRID_W = 64
CHUNK = 128
HEAD_DIM = 64
SG_GROUP_W = 128
WIN_R = 8
WIN_W = 16
ROPE_THETA = 10000.0
EPS = 1e-6
NEG_INF = -1e30
LANES = 128
QK_SCALE = HEAD_DIM ** -0.5

BF16 = jnp.bfloat16
F32 = jnp.float32

VMEM_LIMIT = 56 * 1024 * 1024


def _cparams(sem):
    return pltpu.CompilerParams(dimension_semantics=sem, vmem_limit_bytes=VMEM_LIMIT)


def _sigmoid(x):
    return 1.0 / (1.0 + jnp.exp(-x))


def _silu(x):
    return x * _sigmoid(x)


def _dot(a, b):
    return jnp.dot(a, b, preferred_element_type=F32)


def _dot_nt(a, b):
    return lax.dot_general(a, b, (((1,), (1,)), ((), ())), preferred_element_type=F32)


def _layernorm(x):
    mu = jnp.mean(x, axis=-1, keepdims=True)
    xc = x - mu
    var = jnp.mean(xc * xc, axis=-1, keepdims=True)
    return xc * lax.rsqrt(var + EPS)


def _fold_lanes(x, op):
    out = x[:, :LANES]
    for i in range(1, x.shape[1] // LANES):
        out = op(out, x[:, i * LANES:(i + 1) * LANES])
    return out


def _ada_kernel(c_ref, w_ref, b_ref, o_ref):
    s = _silu(c_ref[...]).astype(BF16)
    o_ref[0] = _dot(s, w_ref[0].astype(BF16)) + b_ref[0]


def _ada_mod(cvec, w_ada, b_ada):
    depth, d, d3 = w_ada.shape
    rows = cvec.shape[0]
    tn = 1024
    return pl.pallas_call(
        _ada_kernel,
        grid=(depth, d3 // tn),
        in_specs=[
            pl.BlockSpec((rows, d), lambda l, j: (0, 0)),
            pl.BlockSpec((1, d, tn), lambda l, j: (l, 0, j)),
            pl.BlockSpec((1, 1, tn), lambda l, j: (l, 0, j)),
        ],
        out_specs=pl.BlockSpec((1, rows, tn), lambda l, j: (l, 0, j)),
        out_shape=jax.ShapeDtypeStruct((depth, rows, d3), F32),
        compiler_params=_cparams(("parallel", "parallel")),
        name="ada_mod",
    )(cvec, w_ada, b_ada.reshape(depth, 1, d3))


def _lambda_kernel(lam_ref, o_ref, *, lam_inits):
    p = lam_ref[...]
    t1 = jnp.sum(p[:, 0, :] * p[:, 1, :], axis=-1, keepdims=True)
    t2 = jnp.sum(p[:, 2, :] * p[:, 3, :], axis=-1, keepdims=True)
    layer = lax.broadcasted_iota(jnp.int32, t1.shape, 0)
    init = jnp.zeros_like(t1)
    for l, v in enumerate(lam_inits):
        init = jnp.where(layer == l, v, init)
    o_ref[...] = jnp.broadcast_to(jnp.exp(t1) - jnp.exp(t2) + init, o_ref.shape)


def _diff_lambdas(lq1, lk1, lq2, lk2, lam_inits):
    depth = lq1.shape[0]
    params = jnp.stack([lq1, lk1, lq2, lk2], axis=1).astype(F32)
    out = pl.pallas_call(
        functools.partial(_lambda_kernel, lam_inits=lam_inits),
        out_shape=jax.ShapeDtypeStruct((depth, LANES), F32),
        name="diff_lambda",
    )(params)
    return out[:, 0]


def _rope(a, c, s_up, s_dn):
    outs = []
    for i in range(a.shape[1] // LANES):
        g = a[:, i * LANES:(i + 1) * LANES]
        outs.append(g * c + pltpu.roll(g, LANES - 16, 1) * s_up + pltpu.roll(g, 16, 1) * s_dn)
    return jnp.concatenate(outs, axis=1)


def _inproj_kernel(*refs, rope, blk):
    if rope:
        x_ref, mod_ref, w_ref, c_ref, su_ref, sd_ref, h_ref, z_ref, hs_ref = refs
    else:
        x_ref, mod_ref, w_ref, h_ref, z_ref, hs_ref = refs
    j = pl.program_id(2)

    @pl.when(j == 0)
    def _():
        xn = _layernorm(x_ref[0])
        h = xn * (1.0 + mod_ref[0, 0, 1:2, :]) + mod_ref[0, 0, 0:1, :]
        hb = h.astype(BF16)
        hs_ref[...] = hb
        h_ref[0] = hb

    acc = _dot(hs_ref[...], w_ref[...])
    is_qb = j == blk["q_b"]
    is_kb = j == blk["k_b"]
    is_qc = j == blk["q_c"]

    if rope:
        @pl.when(is_qb)
        def _():
            z_ref[0] = (_rope(acc, c_ref[...], su_ref[...], sd_ref[...]) * QK_SCALE).astype(z_ref.dtype)

        @pl.when(is_kb)
        def _():
            z_ref[0] = _rope(acc, c_ref[...], su_ref[...], sd_ref[...]).astype(z_ref.dtype)

        @pl.when(is_qc)
        def _():
            z_ref[0] = (acc * QK_SCALE).astype(z_ref.dtype)

        @pl.when(jnp.logical_not(is_qb | is_kb | is_qc))
        def _():
            z_ref[0] = acc.astype(z_ref.dtype)
    else:
        @pl.when(is_qb | is_qc)
        def _():
            z_ref[0] = (acc * QK_SCALE).astype(z_ref.dtype)

        @pl.when(jnp.logical_not(is_qb | is_qc))
        def _():
            z_ref[0] = acc.astype(z_ref.dtype)


def _inproj(x, mod, mod_row0, w, blk, tn, z_dtype, rope_tabs=None):
    B, L, D = x.shape
    d_in = w.shape[1]
    tm = min(L, 1024)
    rope = rope_tabs is not None
    in_specs = [
        pl.BlockSpec((1, tm, D), lambda b, i, j: (b, i, 0)),
        pl.BlockSpec((1, 1, 3, D), lambda b, i, j: (mod_row0 + b, 0, 0, 0)),
        pl.BlockSpec((D, tn), lambda b, i, j: (0, j)),
    ]
    args = [x, mod, w]
    if rope:
        in_specs += [pl.BlockSpec((tm, LANES), lambda b, i, j: (i, 0))] * 3
        args += list(rope_tabs)
    return pl.pallas_call(
        functools.partial(_inproj_kernel, rope=rope, blk=blk),
        grid=(B, L // tm, d_in // tn),
        in_specs=in_specs,
        out_specs=[
            pl.BlockSpec((1, tm, D), lambda b, i, j: (b, i, 0)),
            pl.BlockSpec((1, tm, tn), lambda b, i, j: (b, i, j)),
        ],
        out_shape=[
            jax.ShapeDtypeStruct((B, L, D), BF16),
            jax.ShapeDtypeStruct((B, L, d_in), z_dtype),
        ],
        scratch_shapes=[pltpu.VMEM((tm, D), BF16)],
        compiler_params=_cparams(("parallel", "parallel", "arbitrary")),
        name="inproj_rope" if rope else "inproj",
    )(*args)


def _mixer_a_kernel(u_ref, v_ref, g_ref, ng_ref, nb_ref, ws_ref, bs_ref, o_ref):
    tm, d_a = v_ref.shape[1], v_ref.shape[2]
    n_groups = d_a // SG_GROUP_W
    vn = (_layernorm(v_ref[0].astype(F32)) * ng_ref[...] + nb_ref[...]).astype(BF16)
    for n in range(tm // CHUNK):
        rows = slice(n * CHUNK, (n + 1) * CHUNK)
        sv = jnp.concatenate(
            [_dot(ws_ref[g], vn[rows, g * SG_GROUP_W:(g + 1) * SG_GROUP_W]) for g in range(n_groups)],
            axis=1) + bs_ref[...]
        y = u_ref[0, rows, :].astype(F32) * sv * _silu(g_ref[0, rows, :].astype(F32))
        o_ref[0, rows, :] = y.astype(o_ref.dtype)


def _mixer_a(z, blk, norm_g, norm_b, w_s, b_s_full):
    B, L, _ = z.shape
    d_a = norm_g.shape[-1]
    tm = min(L, 512)
    zspec = lambda name: pl.BlockSpec((1, tm, d_a), lambda b, i, c=blk[name]: (b, i, c))
    const2 = lambda shp: pl.BlockSpec(shp, lambda b, i: (0, 0))
    return pl.pallas_call(
        _mixer_a_kernel,
        grid=(B, L // tm),
        in_specs=[
            zspec("u_a"), zspec("v_a"), zspec("g_a"),
            const2((1, d_a)), const2((1, d_a)),
            pl.BlockSpec(w_s.shape, lambda b, i: (0, 0, 0)),
            const2((CHUNK, d_a)),
        ],
        out_specs=pl.BlockSpec((1, tm, d_a), lambda b, i: (b, i, 0)),
        out_shape=jax.ShapeDtypeStruct((B, L, d_a), BF16),
        compiler_params=_cparams(("parallel", "parallel")),
        name="mixer_a",
    )(z, z, z, norm_g, norm_b, w_s, b_s_full)


def _diff_attn_kernel(*refs, layer, lam_init, tk, n_self, n_ctx):
    if n_ctx:
        (lam_ref, q1_ref, q2_ref, k1_ref, k2_ref, v_ref, g_ref, kc1_ref, kc2_ref, vc_ref,
         sg_ref, o_ref, s_scr) = refs
    else:
        lam_ref, q1_ref, q2_ref, k1_ref, k2_ref, v_ref, g_ref, sg_ref, o_ref, s_scr = refs
    lam = lam_ref[layer]
    tq = q1_ref.shape[1]
    dv = 2 * HEAD_DIM
    lane = lax.broadcasted_iota(jnp.int32, (1, LANES), 1)
    q_pair = (q1_ref[0], q2_ref[0])
    k_self = (k1_ref, k2_ref)

    for hh in range(2):
        head_lanes = (lane < HEAD_DIM) if hh == 0 else (lane >= HEAD_DIM)
        qm = [jnp.where(head_lanes, q, jnp.zeros_like(q)).astype(BF16) for q in q_pair]
        neg = jnp.full((tq, LANES), -jnp.inf, F32)

        def score_self(c, mx):
            out = []
            for m in range(2):
                kch = k_self[m][0, pl.ds(pl.multiple_of(c * tk, tk), tk), :].astype(BF16)
                s = _dot_nt(qm[m], kch)
                s_scr[m, c] = s
                out.append(jnp.maximum(mx[m], _fold_lanes(s, jnp.maximum)))
            return tuple(out)

        mx = lax.fori_loop(0, n_self, score_self, (neg, neg))
        if n_ctx:
            kc = (kc1_ref, kc2_ref)
            mx = list(mx)
            for c in range(n_ctx):
                for m in range(2):
                    s = _dot_nt(qm[m], kc[m][0, 0, c * tk:(c + 1) * tk, :].astype(BF16))
                    s_scr[m, n_self + c] = s
                    mx[m] = jnp.maximum(mx[m], _fold_lanes(s, jnp.maximum))
        row_max = [jnp.max(m_, axis=1, keepdims=True) for m_ in mx]

        def exp_sum(c, ls):
            out = []
            for m in range(2):
                e = jnp.exp(s_scr[m, c] - row_max[m])
                s_scr[m, c] = e
                out.append(ls[m] + _fold_lanes(e, jnp.add))
            return tuple(out)

        zero = jnp.zeros((tq, LANES), F32)
        ls = lax.fori_loop(0, n_self + n_ctx, exp_sum, (zero, zero))
        c1 = 1.0 / jnp.sum(ls[0], axis=1, keepdims=True)
        c2 = lam / jnp.sum(ls[1], axis=1, keepdims=True)

        def weights(c):
            return (s_scr[0, c] * c1 - s_scr[1, c] * c2).astype(BF16)

        def pv_self(c, acc):
            vch = v_ref[0, pl.ds(pl.multiple_of(c * tk, tk), tk), hh * dv:(hh + 1) * dv].astype(BF16)
            return acc + _dot(weights(c), vch)

        acc = lax.fori_loop(0, n_self, pv_self, jnp.zeros((tq, dv), F32))
        for c in range(n_ctx):
            vch = vc_ref[0, 0, c * tk:(c + 1) * tk, hh * dv:(hh + 1) * dv].astype(BF16)
            acc = acc + _dot(weights(n_self + c), vch)

        of = acc * lax.rsqrt(jnp.mean(acc * acc, axis=-1, keepdims=True) + EPS)
        of = of * sg_ref[...] * (1.0 - lam_init)
        y = of * _silu(g_ref[0, :, hh * dv:(hh + 1) * dv].astype(F32))
        o_ref[0, :, hh * dv:(hh + 1) * dv] = y.astype(o_ref.dtype)


def _diff_attn(z, blk, d_b, lam_vec, subln_g, layer, lam_init, cache=None):
    B, L, _ = z.shape
    tq = min(L, 256)
    tk = min(L, 512)
    n_pairs = d_b // (4 * HEAD_DIM)
    col128 = lambda name: blk[name] * (d_b // LANES)
    col256 = lambda name: blk[name] * (d_b // (2 * LANES))
    qspec = lambda m: pl.BlockSpec((1, tq, LANES), lambda b, p, i: (b, i, col128("q_b") + m * n_pairs + p))
    kspec = lambda m: pl.BlockSpec((1, L, LANES), lambda b, p, i: (b, 0, col128("k_b") + m * n_pairs + p))
    in_specs = [
        pl.BlockSpec(memory_space=pltpu.SMEM),
        qspec(0), qspec(1), kspec(0), kspec(1),
        pl.BlockSpec((1, L, 2 * LANES), lambda b, p, i: (b, 0, col256("v_b") + p)),
        pl.BlockSpec((1, tq, 2 * LANES), lambda b, p, i: (b, i, col256("g_b") + p)),
    ]
    args = [lam_vec, z, z, z, z, z, z]
    n_ctx = 0
    if cache is not None:
        ck, cv = cache
        P = ck.shape[2]
        assert P % tk == 0
        n_ctx = P // tk
        in_specs += [
            pl.BlockSpec((1, 1, P, LANES), lambda b, p, i: (b, layer, 0, p)),
            pl.BlockSpec((1, 1, P, LANES), lambda b, p, i: (b, layer, 0, n_pairs + p)),
            pl.BlockSpec((1, 1, P, 2 * LANES), lambda b, p, i: (b, layer, 0, p)),
        ]
        args += [ck, ck, cv]
    in_specs.append(pl.BlockSpec((1, 2 * HEAD_DIM), lambda b, p, i: (0, 0)))
    args.append(subln_g)
    n_self = L // tk
    return pl.pallas_call(
        functools.partial(_diff_attn_kernel, layer=layer, lam_init=lam_init, tk=tk, n_self=n_self, n_ctx=n_ctx),
        grid=(B, n_pairs, L // tq),
        in_specs=in_specs,
        out_specs=pl.BlockSpec((1, tq, 2 * LANES), lambda b, p, i: (b, i, p)),
        out_shape=jax.ShapeDtypeStruct((B, L, d_b), BF16),
        scratch_shapes=[pltpu.VMEM((2, n_self + n_ctx, tq, tk), F32)],
        compiler_params=_cparams(("parallel", "parallel", "parallel")),
        name="diff_attn_latent" if cache is not None else "diff_attn_ctx",
    )(*args)


def _softmax_attn_kernel(q_ref, k_ref, v_ref, g_ref, o_ref):
    lane = lax.broadcasted_iota(jnp.int32, (1, LANES), 1)
    q = q_ref[0]
    k = k_ref[0].astype(BF16)
    v = v_ref[0].astype(BF16)
    outs = []
    for hh in range(2):
        head_lanes = (lane < HEAD_DIM) if hh == 0 else (lane >= HEAD_DIM)
        qm = jnp.where(head_lanes, q, jnp.zeros_like(q)).astype(BF16)
        s = _dot_nt(qm, k)
        e = jnp.exp(s - jnp.max(s, axis=-1, keepdims=True))
        o = _dot(e.astype(BF16), v) * (1.0 / jnp.sum(e, axis=-1, keepdims=True))
        outs.append(o)
    o = jnp.where(lane < HEAD_DIM, outs[0], outs[1])
    o_ref[0] = (o * _silu(g_ref[0].astype(F32))).astype(o_ref.dtype)


def _softmax_attn(z, blk, d_c):
    B, L, _ = z.shape
    n_pairs = d_c // LANES
    spec = lambda name: pl.BlockSpec((1, L, LANES), lambda b, p, c=blk[name] * n_pairs: (b, 0, c + p))
    return pl.pallas_call(
        _softmax_attn_kernel,
        grid=(B, n_pairs),
        in_specs=[spec("q_c"), spec("k_c"), spec("v_c"), spec("g_c")],
        out_specs=pl.BlockSpec((1, L, LANES), lambda b, p: (b, 0, p)),
        out_shape=jax.ShapeDtypeStruct((B, L, d_c), BF16),
        compiler_params=_cparams(("parallel", "parallel")),
        name="softmax_attn_ctx",
    )(z, z, z, z)


def _na_kernel(q_ref, k_ref, v_ref, g_ref, kc_ref, vc_ref, bias_ref, o_ref, *, rows):
    lane = lax.broadcasted_iota(jnp.int32, (1, LANES), 1)
    kc = kc_ref[0, 0]
    vc = vc_ref[0, 0]
    n_loc = WIN_R * GRID_W

    def row(r, carry):
        rs = jnp.clip(r - WIN_R // 2, 0, rows - WIN_R)
        off = rs - r + WIN_R - 1
        q0 = pl.multiple_of(r * GRID_W, GRID_W)
        k0 = pl.multiple_of(rs * GRID_W, GRID_W)
        q = q_ref[0, pl.ds(q0, GRID_W), :]
        kl = k_ref[0, pl.ds(k0, n_loc), :]
        vl = v_ref[0, pl.ds(k0, n_loc), :]
        outs = []
        for hh in range(2):
            head_lanes = (lane < HEAD_DIM) if hh == 0 else (lane >= HEAD_DIM)
            qm = jnp.where(head_lanes, q, jnp.zeros_like(q))
            s_loc = _dot_nt(qm, kl) + bias_ref[0, hh, off]
            s_ctx = _dot_nt(qm, kc)
            m = jnp.maximum(jnp.max(s_loc, axis=-1, keepdims=True), jnp.max(s_ctx, axis=-1, keepdims=True))
            p_loc = jnp.exp(s_loc - m)
            p_ctx = jnp.exp(s_ctx - m)
            denom = jnp.sum(p_loc, axis=-1, keepdims=True) + jnp.sum(p_ctx, axis=-1, keepdims=True)
            o = _dot(p_loc.astype(BF16), vl) + _dot(p_ctx.astype(BF16), vc)
            outs.append(o * (1.0 / denom))
        o = jnp.where(lane < HEAD_DIM, outs[0], outs[1])
        y = o * _silu(g_ref[0, pl.ds(q0, GRID_W), :].astype(F32))
        o_ref[0, pl.ds(q0, GRID_W), :] = y.astype(o_ref.dtype)
        return carry

    lax.fori_loop(0, rows, row, 0)


def _na_attn(z, blk, d_c, nk, nv, bias_tab, layer):
    B, L, _ = z.shape
    rows = L // GRID_W
    assert rows >= WIN_R and L % GRID_W == 0
    P = nk.shape[2]
    n_pairs = d_c // LANES
    spec = lambda name: pl.BlockSpec((1, L, LANES), lambda b, p, c=blk[name] * n_pairs: (b, 0, c + p))
    cspec = pl.BlockSpec((1, 1, P, LANES), lambda b, p: (b, layer, 0, p))
    return pl.pallas_call(
        functools.partial(_na_kernel, rows=rows),
        grid=(B, n_pairs),
        in_specs=[
            spec("q_c"), spec("k_c"), spec("v_c"), spec("g_c"), cspec, cspec,
            pl.BlockSpec((1, 2, WIN_R, GRID_W, WIN_R * GRID_W), lambda b, p: (layer, p, 0, 0, 0)),
        ],
        out_specs=pl.BlockSpec((1, L, LANES), lambda b, p: (b, 0, p)),
        out_shape=jax.ShapeDtypeStruct((B, L, d_c), BF16),
        compiler_params=_cparams(("parallel", "parallel")),
        name="na_attn_latent",
    )(z, z, z, z, nk, nv, bias_tab)


def _merge_kernel(x_ref, h_ref, ya_ref, yb_ref, yc_ref, mod_ref, wmg_ref, bmg_ref, wa_ref, wb_ref, wc_ref,
                  wo_ref, lg_ref, lb_ref, o_ref, *, alpha):
    D = x_ref.shape[2]
    h = h_ref[0]
    m = None
    for i, (y_ref, w_ref) in enumerate(((ya_ref, wa_ref), (yb_ref, wb_ref), (yc_ref, wc_ref))):
        gate = _sigmoid(_dot(h, wmg_ref[:, i * D:(i + 1) * D]) + bmg_ref[:, i * D:(i + 1) * D])
        t = gate * _dot(y_ref[0], w_ref[...])
        m = t if m is None else m + t
    out = _dot(m.astype(BF16), wo_ref[...])
    t = alpha * x_ref[0] + mod_ref[0, 0, 2:3, :] * out
    o_ref[0] = _layernorm(t) * lg_ref[...] + lb_ref[...]


def _merge(x, h, ya, yb, yc, mod, mod_row0, wmg, bmg, wa, wb, wc, wo, lg, lb, alpha):
    B, L, D = x.shape
    tm = min(L, 512)
    tok = lambda w: pl.BlockSpec((1, tm, w), lambda b, i: (b, i, 0))
    const = lambda a: pl.BlockSpec(a.shape, lambda b, i: (0,) * a.ndim, pipeline_mode=pl.Buffered(1))
    return pl.pallas_call(
        functools.partial(_merge_kernel, alpha=alpha),
        grid=(B, L // tm),
        in_specs=[
            tok(D), tok(D), tok(ya.shape[2]), tok(yb.shape[2]), tok(yc.shape[2]),
            pl.BlockSpec((1, 1, 3, D), lambda b, i: (mod_row0 + b, 0, 0, 0)),
            const(wmg), const(bmg), const(wa), const(wb), const(wc), const(wo), const(lg), const(lb),
        ],
        out_specs=tok(D),
        out_shape=jax.ShapeDtypeStruct((B, L, D), F32),
        compiler_params=_cparams(("parallel", "parallel")),
        name="merge_postnorm",
    )(x, h, ya, yb, yc, mod, wmg, bmg, wa, wb, wc, wo, lg, lb)


def _rope_tables(n_tok):
    half = HEAD_DIM // 2
    quarter = HEAD_DIM // 4
    t = jnp.arange(n_tok)
    inv = 1.0 / (ROPE_THETA ** (jnp.arange(0, half, 2, dtype=F32) / half))
    ang = jnp.stack([(t // GRID_W).astype(F32)[:, None] * inv, (t % GRID_W).astype(F32)[:, None] * inv], axis=1)
    cos = jnp.cos(ang)[:, :, None, :]
    sin = jnp.sin(ang)[:, :, None, :]
    zero = jnp.zeros_like(sin)
    shape = (n_tok, 2, 2, quarter)
    c = jnp.broadcast_to(cos, shape).reshape(n_tok, HEAD_DIM)
    s_up = jnp.concatenate([-sin, zero], axis=2).reshape(n_tok, HEAD_DIM)
    s_dn = jnp.concatenate([zero, sin], axis=2).reshape(n_tok, HEAD_DIM)
    rep = LANES // HEAD_DIM
    return tuple(jnp.tile(a, (1, rep)) for a in (c, s_up, s_dn))


def _na_bias_tables(rel_bias):
    cpos = np.arange(GRID_W)
    cstart = np.clip(cpos - WIN_W // 2, 0, GRID_W - WIN_W)
    colmask = (cpos[None, :] >= cstart[:, None]) & (cpos[None, :] < cstart[:, None] + WIN_W)
    dx_idx = np.clip(cpos[None, :] - cpos[:, None], -(WIN_W - 1), WIN_W - 1) + WIN_W - 1
    dy = np.arange(WIN_R)[:, None] + np.arange(WIN_R)[None, :]
    tab = rel_bias[:, :, dy[:, :, None, None], dx_idx[None, None, :, :]]
    tab = jnp.where(colmask[None, None, None, None], tab, NEG_INF)
    tab = tab.transpose(0, 1, 2, 4, 3, 5)
    return tab.reshape(*tab.shape[:4], WIN_R * GRID_W).astype(F32)


def kernel(x_prompt, x_sample, c, cache_diff_k, cache_diff_v, cache_na_k, cache_na_v, c_ctx, w_ada, b_ada, w_in, sg_norm_g, sg_norm_b, w_spatial, b_spatial, lambda_q1, lambda_k1, lambda_q2, lambda_k2, diff_subln_g, na_rel_bias, w_br_a, w_br_b, w_br_c, w_mgate, b_mgate, w_out, ln_g, ln_b):
    depth, D, d_in = w_in.shape
    batch, seq, _ = x_prompt.shape
    dec_batch, dec_seq, _ = x_sample.shape
    past = cache_diff_k.shape[4]
    d_a, d_b, d_c = w_br_a.shape[1], w_br_b.shape[1], w_br_c.shape[1]
    h_b = d_b // (2 * HEAD_DIM)
    h_c = d_c // HEAD_DIM
    tn = d_a
    assert d_a == d_b == d_c and d_in == 11 * tn and tn % (2 * LANES) == 0
    assert d_a // SG_GROUP_W == w_spatial.shape[1] and w_spatial.shape[2] == CHUNK
    names = ("u_a", "v_a", "g_a", "q_b", "k_b", "v_b", "g_b", "q_c", "k_c", "v_c", "g_c")
    blk = {n: i for i, n in enumerate(names)}
    alpha = (2 * depth) ** 0.25
    lam_inits = tuple(0.8 - 0.6 * math.exp(-0.3 * l) for l in range(depth))

    mod_rows = -(-(1 + dec_batch) // 8) * 8
    cvec = jnp.zeros((mod_rows, D), F32).at[0].set(c_ctx).at[1:1 + dec_batch].set(c)
    mod = _ada_mod(cvec, w_ada, b_ada).reshape(depth * mod_rows, 1, 3, D)
    lam_vec = _diff_lambdas(lambda_q1, lambda_k1, lambda_q2, lambda_k2, lam_inits)

    w_in_b = w_in.astype(BF16)
    w_s_b = w_spatial.astype(BF16)
    b_s_full = jnp.repeat(jnp.swapaxes(b_spatial, 1, 2), SG_GROUP_W, axis=2)
    wmg_b, wa_b, wb_b, wc_b, wo_b = (w.astype(BF16) for w in (w_mgate, w_br_a, w_br_b, w_br_c, w_out))
    ck = cache_diff_k.transpose(0, 1, 4, 2, 3, 5).reshape(dec_batch, depth, past, d_b).astype(BF16)
    cv = cache_diff_v.transpose(0, 1, 3, 2, 4).reshape(dec_batch, depth, past, d_b).astype(BF16)
    nk = cache_na_k.transpose(0, 1, 3, 2, 4).reshape(dec_batch, depth, past, d_c).astype(BF16)
    nv = cache_na_v.transpose(0, 1, 3, 2, 4).reshape(dec_batch, depth, past, d_c).astype(BF16)
    rope_tabs = _rope_tables(dec_seq)
    bias_tab = _na_bias_tables(na_rel_bias)

    xp = x_prompt.reshape(1, batch * seq, D)
    xs = x_sample
    new_dk, new_dv, new_nk, new_nv = [], [], [], []
    for l in range(depth):
        row2 = lambda a: a[l].reshape(1, -1)
        merge_w = (wmg_b[l], row2(b_mgate), wa_b[l], wb_b[l], wc_b[l], wo_b[l], row2(ln_g), row2(ln_b))
        sg = (row2(sg_norm_g), row2(sg_norm_b), w_s_b[l], b_s_full[l])

        h, z = _inproj(xp, mod, l * mod_rows, w_in_b[l], blk, tn, F32)
        zc = z.reshape(batch, seq, d_in)
        ya = _mixer_a(zc, blk, *sg)
        yb = _diff_attn(zc, blk, d_b, lam_vec, row2(diff_subln_g), l, lam_inits[l])
        yc = _softmax_attn(zc, blk, d_c)
        flat = lambda a: a.reshape(1, batch * seq, a.shape[-1])
        xp = _merge(xp, h, flat(ya), flat(yb), flat(yc), mod, l * mod_rows, *merge_w, alpha)
        col = lambda n: zc[:, :, blk[n] * tn:(blk[n] + 1) * tn]
        new_dk.append(col("k_b").reshape(batch, seq, 2, h_b, HEAD_DIM).transpose(0, 2, 3, 1, 4))
        new_dv.append(col("v_b").reshape(batch, seq, h_b, 2 * HEAD_DIM).transpose(0, 2, 1, 3))
        new_nk.append(col("k_c").reshape(batch, seq, h_c, HEAD_DIM).transpose(0, 2, 1, 3))
        new_nv.append(col("v_c").reshape(batch, seq, h_c, HEAD_DIM).transpose(0, 2, 1, 3))

        h, z = _inproj(xs, mod, l * mod_rows + 1, w_in_b[l], blk, tn, BF16, rope_tabs)
        ya = _mixer_a(z, blk, *sg)
        yb = _diff_attn(z, blk, d_b, lam_vec, row2(diff_subln_g), l, lam_inits[l], cache=(ck, cv))
        yc = _na_attn(z, blk, d_c, nk, nv, bias_tab, l)
        xs = _merge(xs, h, ya, yb, yc, mod, l * mod_rows + 1, *merge_w, alpha)

    return (xp.reshape(batch, seq, D), xs,
            jnp.stack(new_dk, axis=1), jnp.stack(new_dv, axis=1),
            jnp.stack(new_nk, axis=1), jnp.stack(new_nv, axis=1))
```

```python
import functools
import math

import jax
import jax.numpy as jnp
import numpy as np
from jax import lax
from jax.experimental import pallas as pl
from jax.experimental.pallas import tpu as pltpu

GRID_W = 64
CHUNK = 128
HEAD_DIM = 64
SG_GROUP_W = 128
WIN_R = 8
WIN_W = 16
NA_ROWS_PER_BLOCK = 4
ROPE_THETA = 10000.0
EPS = 1e-6
NEG_INF = -1e30
LANES = 128
LOG2E = math.log2(math.e)
Q_SCALE = HEAD_DIM ** -0.5 * LOG2E

BF16 = jnp.bfloat16
F32 = jnp.float32

VMEM_LIMIT = 56 * 1024 * 1024


def _cparams(sem):
    return pltpu.CompilerParams(dimension_semantics=sem, vmem_limit_bytes=VMEM_LIMIT)


def _sigmoid(x):
    return 1.0 / (1.0 + jnp.exp(-x))


def _silu(x):
    return x * _sigmoid(x)


def _dot(a, b):
    return jnp.dot(a, b, preferred_element_type=F32)


def _dot_nt(a, b):
    return lax.dot_general(a, b, (((1,), (1,)), ((), ())), preferred_element_type=F32)


def _layernorm(x):
    mu = jnp.mean(x, axis=-1, keepdims=True)
    xc = x - mu
    var = jnp.mean(xc * xc, axis=-1, keepdims=True)
    return xc * lax.rsqrt(var + EPS)


def _fold_lanes(x, op):
    out = x[:, :LANES]
    for i in range(1, x.shape[1] // LANES):
        out = op(out, x[:, i * LANES:(i + 1) * LANES])
    return out


def _ada_kernel(c_ref, w_ref, b_ref, o_ref):
    s = _silu(c_ref[...]).astype(BF16)
    o_ref[0] = _dot(s, w_ref[0].astype(BF16)) + b_ref[0]


def _ada_mod(cvec, w_ada, b_ada):
    depth, d, d3 = w_ada.shape
    rows = cvec.shape[0]
    tn = 1024
    return pl.pallas_call(
        _ada_kernel,
        grid=(depth, d3 // tn),
        in_specs=[
            pl.BlockSpec((rows, d), lambda l, j: (0, 0)),
            pl.BlockSpec((1, d, tn), lambda l, j: (l, 0, j)),
            pl.BlockSpec((1, 1, tn), lambda l, j: (l, 0, j)),
        ],
        out_specs=pl.BlockSpec((1, rows, tn), lambda l, j: (l, 0, j)),
        out_shape=jax.ShapeDtypeStruct((depth, rows, d3), F32),
        compiler_params=_cparams(("parallel", "parallel")),
        name="ada_mod",
    )(cvec, w_ada, b_ada.reshape(depth, 1, d3))


def _lambda_kernel(lam_ref, o_ref, *, lam_inits):
    p = lam_ref[...]
    t1 = jnp.sum(p[:, 0, :] * p[:, 1, :], axis=-1, keepdims=True)
    t2 = jnp.sum(p[:, 2, :] * p[:, 3, :], axis=-1, keepdims=True)
    layer = lax.broadcasted_iota(jnp.int32, t1.shape, 0)
    init = jnp.zeros_like(t1)
    for l, v in enumerate(lam_inits):
        init = jnp.where(layer == l, v, init)
    o_ref[...] = jnp.broadcast_to(jnp.exp(t1) - jnp.exp(t2) + init, o_ref.shape)


def _diff_lambdas(lq1, lk1, lq2, lk2, lam_inits):
    depth = lq1.shape[0]
    params = jnp.stack([lq1, lk1, lq2, lk2], axis=1).astype(F32)
    out = pl.pallas_call(
        functools.partial(_lambda_kernel, lam_inits=lam_inits),
        out_shape=jax.ShapeDtypeStruct((depth, LANES), F32),
        name="diff_lambda",
    )(params)
    return out[:, 0]


def _rope(a, c, s_up, s_dn):
    outs = []
    for i in range(a.shape[1] // LANES):
        g = a[:, i * LANES:(i + 1) * LANES]
        outs.append(g * c + pltpu.roll(g, LANES - 16, 1) * s_up + pltpu.roll(g, 16, 1) * s_dn)
    return jnp.concatenate(outs, axis=1)


def _inproj_kernel(*refs, rope, blk):
    if rope:
        x_ref, mod_ref, w_ref, c_ref, su_ref, sd_ref, h_ref, z_ref, hs_ref = refs
    else:
        x_ref, mod_ref, w_ref, h_ref, z_ref, hs_ref = refs
    j = pl.program_id(2)

    @pl.when(j == 0)
    def _():
        xn = _layernorm(x_ref[0])
        h = xn * (1.0 + mod_ref[0, 0, 1:2, :]) + mod_ref[0, 0, 0:1, :]
        hb = h.astype(BF16)
        hs_ref[...] = hb
        h_ref[0] = hb

    acc = _dot(hs_ref[...], w_ref[...])
    is_qb = j == blk["q_b"]
    is_kb = j == blk["k_b"]
    is_qc = j == blk["q_c"]

    if rope:
        @pl.when(is_qb)
        def _():
            z_ref[0] = (_rope(acc, c_ref[...], su_ref[...], sd_ref[...]) * Q_SCALE).astype(z_ref.dtype)

        @pl.when(is_kb)
        def _():
            z_ref[0] = _rope(acc, c_ref[...], su_ref[...], sd_ref[...]).astype(z_ref.dtype)

        @pl.when(is_qc)
        def _():
            z_ref[0] = (acc * Q_SCALE).astype(z_ref.dtype)

        @pl.when(jnp.logical_not(is_qb | is_kb | is_qc))
        def _():
            z_ref[0] = acc.astype(z_ref.dtype)
    else:
        @pl.when(is_qb | is_qc)
        def _():
            z_ref[0] = (acc * Q_SCALE).astype(z_ref.dtype)

        @pl.when(jnp.logical_not(is_qb | is_qc))
        def _():
            z_ref[0] = acc.astype(z_ref.dtype)


def _inproj(x, mod, mod_row0, w, blk, tn, z_dtype, rope_tabs=None):
    B, L, D = x.shape
    d_in = w.shape[1]
    tm = min(L, 1024)
    rope = rope_tabs is not None
    in_specs = [
        pl.BlockSpec((1, tm, D), lambda b, i, j: (b, i, 0)),
        pl.BlockSpec((1, 1, 3, D), lambda b, i, j: (mod_row0 + b, 0, 0, 0)),
        pl.BlockSpec((D, tn), lambda b, i, j: (0, j)),
    ]
    args = [x, mod, w]
    if rope:
        in_specs += [pl.BlockSpec((tm, LANES), lambda b, i, j: (i, 0))] * 3
        args += list(rope_tabs)
    return pl.pallas_call(
        functools.partial(_inproj_kernel, rope=rope, blk=blk),
        grid=(B, L // tm, d_in // tn),
        in_specs=in_specs,
        out_specs=[
            pl.BlockSpec((1, tm, D), lambda b, i, j: (b, i, 0)),
            pl.BlockSpec((1, tm, tn), lambda b, i, j: (b, i, j)),
        ],
        out_shape=[
            jax.ShapeDtypeStruct((B, L, D), BF16),
            jax.ShapeDtypeStruct((B, L, d_in), z_dtype),
        ],
        scratch_shapes=[pltpu.VMEM((tm, D), BF16)],
        compiler_params=_cparams(("parallel", "parallel", "arbitrary")),
        name="inproj_rope" if rope else "inproj",
    )(*args)


def _mixer_a_kernel(u_ref, v_ref, g_ref, ng_ref, nb_ref, ws_ref, bs_ref, o_ref):
    tm, d_a = v_ref.shape[1], v_ref.shape[2]
    n_groups = d_a // SG_GROUP_W
    vn = (_layernorm(v_ref[0].astype(F32)) * ng_ref[...] + nb_ref[...]).astype(BF16)
    for n in range(tm // CHUNK):
        rows = slice(n * CHUNK, (n + 1) * CHUNK)
        sv = jnp.concatenate(
            [_dot(ws_ref[g], vn[rows, g * SG_GROUP_W:(g + 1) * SG_GROUP_W]) for g in range(n_groups)],
            axis=1) + bs_ref[...]
        y = u_ref[0, rows, :].astype(F32) * sv * _silu(g_ref[0, rows, :].astype(F32))
        o_ref[0, rows, :] = y.astype(o_ref.dtype)


def _mixer_a(z, blk, norm_g, norm_b, w_s, b_s_full):
    B, L, _ = z.shape
    d_a = norm_g.shape[-1]
    tm = min(L, 512)
    zspec = lambda name: pl.BlockSpec((1, tm, d_a), lambda b, i, c=blk[name]: (b, i, c))
    const2 = lambda shp: pl.BlockSpec(shp, lambda b, i: (0, 0))
    return pl.pallas_call(
        _mixer_a_kernel,
        grid=(B, L // tm),
        in_specs=[
            zspec("u_a"), zspec("v_a"), zspec("g_a"),
            const2((1, d_a)), const2((1, d_a)),
            pl.BlockSpec(w_s.shape, lambda b, i: (0, 0, 0)),
            const2((CHUNK, d_a)),
        ],
        out_specs=pl.BlockSpec((1, tm, d_a), lambda b, i: (b, i, 0)),
        out_shape=jax.ShapeDtypeStruct((B, L, d_a), BF16),
        compiler_params=_cparams(("parallel", "parallel")),
        name="mixer_a",
    )(z, z, z, norm_g, norm_b, w_s, b_s_full)


def _diff_attn_kernel(*refs, layer, lam_init, tk, n_self, n_ctx):
    if n_ctx:
        (lam_ref, q1_ref, q2_ref, k1_ref, k2_ref, v_ref, g_ref, kc1_ref, kc2_ref, vc_ref,
         sg_ref, o_ref) = refs
    else:
        lam_ref, q1_ref, q2_ref, k1_ref, k2_ref, v_ref, g_ref, sg_ref, o_ref = refs
    lam = lam_ref[layer]
    tq = q1_ref.shape[1]
    dv = 2 * HEAD_DIM
    lane = lax.broadcasted_iota(jnp.int32, (1, LANES), 1)
    q_pair = (q1_ref[0], q2_ref[0])
    ones = jnp.ones((tk, LANES), BF16)

    chunks = [(k1_ref.at[0, c * tk:(c + 1) * tk, :], k2_ref.at[0, c * tk:(c + 1) * tk, :],
               v_ref.at[0, c * tk:(c + 1) * tk, :]) for c in range(n_self)]
    if n_ctx:
        chunks += [(kc1_ref.at[0, 0, c * tk:(c + 1) * tk, :], kc2_ref.at[0, 0, c * tk:(c + 1) * tk, :],
                    vc_ref.at[0, 0, c * tk:(c + 1) * tk, :]) for c in range(n_ctx)]

    for hh in range(2):
        head_lanes = (lane < HEAD_DIM) if hh == 0 else (lane >= HEAD_DIM)
        qm = [jnp.where(head_lanes, q, jnp.zeros_like(q)).astype(BF16) for q in q_pair]
        state = [(jnp.full((tq, 1), -jnp.inf, F32), jnp.zeros((tq, 2 * dv), F32)) for _ in range(2)]
        for k1c, k2c, vc in chunks:
            vaug = jnp.concatenate([vc[:, hh * dv:(hh + 1) * dv].astype(BF16), ones], axis=1)
            for m, kc in enumerate((k1c, k2c)):
                mx, acc = state[m]
                s = _dot_nt(qm[m], kc[...].astype(BF16))
                mx_new = jnp.maximum(mx, jnp.max(s, axis=-1, keepdims=True))
                p = jnp.exp2(s - mx_new).astype(BF16)
                state[m] = (mx_new, jnp.exp2(mx - mx_new) * acc + _dot(p, vaug))

        (_, a1), (_, a2) = state
        o = a1[:, :dv] * (1.0 / a1[:, dv:dv + 1]) - a2[:, :dv] * (lam / a2[:, dv:dv + 1])
        of = o * lax.rsqrt(jnp.mean(o * o, axis=-1, keepdims=True) + EPS)
        of = of * sg_ref[...] * (1.0 - lam_init)
        y = of * _silu(g_ref[0, :, hh * dv:(hh + 1) * dv].astype(F32))
        o_ref[0, :, hh * dv:(hh + 1) * dv] = y.astype(o_ref.dtype)


def _diff_attn(z, blk, d_b, lam_vec, subln_g, layer, lam_init, cache=None):
    B, L, _ = z.shape
    tq = min(L, 256)
    tk = min(L, 512)
    n_pairs = d_b // (4 * HEAD_DIM)
    col128 = lambda name: blk[name] * (d_b // LANES)
    col256 = lambda name: blk[name] * (d_b // (2 * LANES))
    qspec = lambda m: pl.BlockSpec((1, tq, LANES), lambda b, p, i: (b, i, col128("q_b") + m * n_pairs + p))
    kspec = lambda m: pl.BlockSpec((1, L, LANES), lambda b, p, i: (b, 0, col128("k_b") + m * n_pairs + p))
    in_specs = [
        pl.BlockSpec(memory_space=pltpu.SMEM),
        qspec(0), qspec(1), kspec(0), kspec(1),
        pl.BlockSpec((1, L, 2 * LANES), lambda b, p, i: (b, 0, col256("v_b") + p)),
        pl.BlockSpec((1, tq, 2 * LANES), lambda b, p, i: (b, i, col256("g_b") + p)),
    ]
    args = [lam_vec, z, z, z, z, z, z]
    n_ctx = 0
    if cache is not None:
        ck, cv = cache
        P = ck.shape[2]
        assert P % tk == 0
        n_ctx = P // tk
        in_specs += [
            pl.BlockSpec((1, 1, P, LANES), lambda b, p, i: (b, layer, 0, p)),
            pl.BlockSpec((1, 1, P, LANES), lambda b, p, i: (b, layer, 0, n_pairs + p)),
            pl.BlockSpec((1, 1, P, 2 * LANES), lambda b, p, i: (b, layer, 0, p)),
        ]
        args += [ck, ck, cv]
    in_specs.append(pl.BlockSpec((1, 2 * HEAD_DIM), lambda b, p, i: (0, 0)))
    args.append(subln_g)
    n_self = L // tk
    return pl.pallas_call(
        functools.partial(_diff_attn_kernel, layer=layer, lam_init=lam_init, tk=tk, n_self=n_self, n_ctx=n_ctx),
        grid=(B, n_pairs, L // tq),
        in_specs=in_specs,
        out_specs=pl.BlockSpec((1, tq, 2 * LANES), lambda b, p, i: (b, i, p)),
        out_shape=jax.ShapeDtypeStruct((B, L, d_b), BF16),
        compiler_params=_cparams(("parallel", "parallel", "parallel")),
        name="diff_attn_latent" if cache is not None else "diff_attn_ctx",
    )(*args)


def _softmax_attn_kernel(q_ref, k_ref, v_ref, g_ref, o_ref):
    lane = lax.broadcasted_iota(jnp.int32, (1, LANES), 1)
    q = q_ref[0]
    k = k_ref[0].astype(BF16)
    v = v_ref[0].astype(BF16)
    outs = []
    for hh in range(2):
        head_lanes = (lane < HEAD_DIM) if hh == 0 else (lane >= HEAD_DIM)
        qm = jnp.where(head_lanes, q, jnp.zeros_like(q)).astype(BF16)
        s = _dot_nt(qm, k)
        e = jnp.exp2(s - jnp.max(s, axis=-1, keepdims=True))
        o = _dot(e.astype(BF16), v) * (1.0 / jnp.sum(e, axis=-1, keepdims=True))
        outs.append(o)
    o = jnp.where(lane < HEAD_DIM, outs[0], outs[1])
    o_ref[0] = (o * _silu(g_ref[0].astype(F32))).astype(o_ref.dtype)


def _softmax_attn(z, blk, d_c):
    B, L, _ = z.shape
    n_pairs = d_c // LANES
    spec = lambda name: pl.BlockSpec((1, L, LANES), lambda b, p, c=blk[name] * n_pairs: (b, 0, c + p))
    return pl.pallas_call(
        _softmax_attn_kernel,
        grid=(B, n_pairs),
        in_specs=[spec("q_c"), spec("k_c"), spec("v_c"), spec("g_c")],
        out_specs=pl.BlockSpec((1, L, LANES), lambda b, p: (b, 0, p)),
        out_shape=jax.ShapeDtypeStruct((B, L, d_c), BF16),
        compiler_params=_cparams(("parallel", "parallel")),
        name="softmax_attn_ctx",
    )(z, z, z, z)


def _na_kernel(q_ref, k_ref, v_ref, g_ref, kc_ref, vc_ref, bias_ref, o_ref, *, rows):
    lane = lax.broadcasted_iota(jnp.int32, (1, LANES), 1)
    n_loc = WIN_R * GRID_W
    P = kc_ref.shape[2]
    kc = kc_ref[0, 0]
    vc_aug = jnp.concatenate([vc_ref[0, 0], jnp.ones((P, LANES), BF16)], axis=1)
    ones_loc = jnp.ones((n_loc, LANES), BF16)
    tq = NA_ROWS_PER_BLOCK * GRID_W

    def block(i, carry):
        q0 = pl.multiple_of(i * tq, tq)
        q = q_ref[0, pl.ds(q0, tq), :]
        outs = []
        for hh in range(2):
            head_lanes = (lane < HEAD_DIM) if hh == 0 else (lane >= HEAD_DIM)
            qm = jnp.where(head_lanes, q, jnp.zeros_like(q))
            s_ctx = _dot_nt(qm, kc)
            m_ctx = jnp.max(s_ctx, axis=-1, keepdims=True)
            o_rows, p_ctx_rows = [], []
            for j in range(NA_ROWS_PER_BLOCK):
                r = i * NA_ROWS_PER_BLOCK + j
                rs = jnp.clip(r - WIN_R // 2, 0, rows - WIN_R)
                off = rs - r + WIN_R - 1
                k0 = pl.multiple_of(rs * GRID_W, GRID_W)
                kl = k_ref[0, pl.ds(k0, n_loc), :]
                vl_aug = jnp.concatenate([v_ref[0, pl.ds(k0, n_loc), :], ones_loc], axis=1)
                rsl = slice(j * GRID_W, (j + 1) * GRID_W)
                s_loc = _dot_nt(qm[rsl], kl) + bias_ref[0, hh, off]
                m = jnp.maximum(jnp.max(s_loc, axis=-1, keepdims=True), m_ctx[rsl])
                p_ctx_rows.append(jnp.exp2(s_ctx[rsl] - m).astype(BF16))
                o_rows.append(_dot(jnp.exp2(s_loc - m).astype(BF16), vl_aug))
            o = jnp.concatenate(o_rows, axis=0) + _dot(jnp.concatenate(p_ctx_rows, axis=0), vc_aug)
            outs.append(o[:, :LANES] * (1.0 / o[:, LANES:LANES + 1]))
        o = jnp.where(lane < HEAD_DIM, outs[0], outs[1])
        y = o * _silu(g_ref[0, pl.ds(q0, tq), :].astype(F32))
        o_ref[0, pl.ds(q0, tq), :] = y.astype(o_ref.dtype)
        return carry

    lax.fori_loop(0, rows // NA_ROWS_PER_BLOCK, block, 0, unroll=2)


def _na_attn(z, blk, d_c, nk, nv, bias_tab, layer):
    B, L, _ = z.shape
    rows = L // GRID_W
    assert rows >= WIN_R and L % GRID_W == 0 and rows % (2 * NA_ROWS_PER_BLOCK) == 0
    P = nk.shape[2]
    n_pairs = d_c // LANES
    spec = lambda name: pl.BlockSpec((1, L, LANES), lambda b, p, c=blk[name] * n_pairs: (b, 0, c + p))
    cspec = pl.BlockSpec((1, 1, P, LANES), lambda b, p: (b, layer, 0, p))
    return pl.pallas_call(
        functools.partial(_na_kernel, rows=rows),
        grid=(B, n_pairs),
        in_specs=[
            spec("q_c"), spec("k_c"), spec("v_c"), spec("g_c"), cspec, cspec,
            pl.BlockSpec((1, 2, WIN_R, GRID_W, WIN_R * GRID_W), lambda b, p: (layer, p, 0, 0, 0)),
        ],
        out_specs=pl.BlockSpec((1, L, LANES), lambda b, p: (b, 0, p)),
        out_shape=jax.ShapeDtypeStruct((B, L, d_c), BF16),
        compiler_params=_cparams(("parallel", "parallel")),
        name="na_attn_latent",
    )(z, z, z, z, nk, nv, bias_tab)


def _merge_kernel(x_ref, h_ref, ya_ref, yb_ref, yc_ref, mod_ref, wmg_ref, bmg_ref, wa_ref, wb_ref, wc_ref,
                  wo_ref, lg_ref, lb_ref, o_ref, *, alpha):
    D = x_ref.shape[2]
    h = h_ref[0]
    m = None
    for i, (y_ref, w_ref) in enumerate(((ya_ref, wa_ref), (yb_ref, wb_ref), (yc_ref, wc_ref))):
        gate = _sigmoid(_dot(h, wmg_ref[:, i * D:(i + 1) * D]) + bmg_ref[:, i * D:(i + 1) * D])
        t = gate * _dot(y_ref[0], w_ref[...])
        m = t if m is None else m + t
    out = _dot(m.astype(BF16), wo_ref[...])
    t = alpha * x_ref[0] + mod_ref[0, 0, 2:3, :] * out
    o_ref[0] = _layernorm(t) * lg_ref[...] + lb_ref[...]


def _merge(x, h, ya, yb, yc, mod, mod_row0, wmg, bmg, wa, wb, wc, wo, lg, lb, alpha):
    B, L, D = x.shape
    tm = min(L, 512)
    tok = lambda w: pl.BlockSpec((1, tm, w), lambda b, i: (b, i, 0))
    const = lambda a: pl.BlockSpec(a.shape, lambda b, i: (0,) * a.ndim, pipeline_mode=pl.Buffered(1))
    return pl.pallas_call(
        functools.partial(_merge_kernel, alpha=alpha),
        grid=(B, L // tm),
        in_specs=[
            tok(D), tok(D), tok(ya.shape[2]), tok(yb.shape[2]), tok(yc.shape[2]),
            pl.BlockSpec((1, 1, 3, D), lambda b, i: (mod_row0 + b, 0, 0, 0)),
            const(wmg), const(bmg), const(wa), const(wb), const(wc), const(wo), const(lg), const(lb),
        ],
        out_specs=tok(D),
        out_shape=jax.ShapeDtypeStruct((B, L, D), F32),
        compiler_params=_cparams(("parallel", "parallel")),
        name="merge_postnorm",
    )(x, h, ya, yb, yc, mod, wmg, bmg, wa, wb, wc, wo, lg, lb)


def _rope_tables(n_tok):
    half = HEAD_DIM // 2
    quarter = HEAD_DIM // 4
    t = jnp.arange(n_tok)
    inv = 1.0 / (ROPE_THETA ** (jnp.arange(0, half, 2, dtype=F32) / half))
    ang = jnp.stack([(t // GRID_W).astype(F32)[:, None] * inv, (t % GRID_W).astype(F32)[:, None] * inv], axis=1)
    cos = jnp.cos(ang)[:, :, None, :]
    sin = jnp.sin(ang)[:, :, None, :]
    zero = jnp.zeros_like(sin)
    shape = (n_tok, 2, 2, quarter)
    c = jnp.broadcast_to(cos, shape).reshape(n_tok, HEAD_DIM)
    s_up = jnp.concatenate([-sin, zero], axis=2).reshape(n_tok, HEAD_DIM)
    s_dn = jnp.concatenate([zero, sin], axis=2).reshape(n_tok, HEAD_DIM)
    rep = LANES // HEAD_DIM
    return tuple(jnp.tile(a, (1, rep)) for a in (c, s_up, s_dn))


def _na_bias_kernel(rb_ref, o_ref):
    shape = (GRID_W, LANES)
    qcol = lax.broadcasted_iota(jnp.int32, shape, 0)
    lane = lax.broadcasted_iota(jnp.int32, shape, 1)
    kcol = lane & (GRID_W - 1)
    cstart = jnp.clip(qcol - WIN_W // 2, 0, GRID_W - WIN_W)
    in_window = (kcol >= cstart) & (kcol < cstart + WIN_W)

    def toeplitz(y, lane0):
        row = jnp.broadcast_to(rb_ref[0, y:y + 1, :], shape)
        return pltpu.roll(row, (lane0 - (WIN_W - 1)) % LANES, 1, stride=1, stride_axis=0)

    n_dy = 2 * WIN_R - 1
    t_lo = [toeplitz(y, 0) for y in range(n_dy)]
    t_hi = [toeplitz(y, GRID_W) for y in range(n_dy)]
    for off in range(WIN_R):
        for jp in range(WIN_R // 2):
            y = off + 2 * jp
            tile = jnp.where(lane < GRID_W, t_lo[y], t_hi[y + 1]) * LOG2E
            o_ref[0, 0, off, :, jp * LANES:(jp + 1) * LANES] = jnp.where(in_window, tile, NEG_INF)


def _na_bias_tables(rel_bias):
    depth, n_heads, n_dy, n_dx = rel_bias.shape
    assert n_dy == 2 * WIN_R - 1 and n_dx == 2 * WIN_W - 1 and WIN_R % 2 == 0
    rb = jnp.pad(rel_bias.astype(F32), ((0, 0), (0, 0), (0, 16 - n_dy), (0, LANES - n_dx)))
    return pl.pallas_call(
        _na_bias_kernel,
        grid=(depth, n_heads),
        in_specs=[pl.BlockSpec((1, 16, LANES), lambda l, h: (l * n_heads + h, 0, 0))],
        out_specs=pl.BlockSpec((1, 1, WIN_R, GRID_W, WIN_R * GRID_W), lambda l, h: (l, h, 0, 0, 0)),
        out_shape=jax.ShapeDtypeStruct((depth, n_heads, WIN_R, GRID_W, WIN_R * GRID_W), F32),
        compiler_params=_cparams(("parallel", "parallel")),
        name="na_bias_table",
    )(rb.reshape(depth * n_heads, 16, LANES))


def kernel(x_prompt, x_sample, c, cache_diff_k, cache_diff_v, cache_na_k, cache_na_v, c_ctx, w_ada, b_ada, w_in, sg_norm_g, sg_norm_b, w_spatial, b_spatial, lambda_q1, lambda_k1, lambda_q2, lambda_k2, diff_subln_g, na_rel_bias, w_br_a, w_br_b, w_br_c, w_mgate, b_mgate, w_out, ln_g, ln_b):
    depth, D, d_in = w_in.shape
    batch, seq, _ = x_prompt.shape
    dec_batch, dec_seq, _ = x_sample.shape
    past = cache_diff_k.shape[4]
    d_a, d_b, d_c = w_br_a.shape[1], w_br_b.shape[1], w_br_c.shape[1]
    h_b = d_b // (2 * HEAD_DIM)
    h_c = d_c // HEAD_DIM
    tn = d_a
    assert d_a == d_b == d_c and d_in == 11 * tn and tn % (2 * LANES) == 0
    assert d_a // SG_GROUP_W == w_spatial.shape[1] and w_spatial.shape[2] == CHUNK
    names = ("u_a", "v_a", "g_a", "q_b", "k_b", "v_b", "g_b", "q_c", "k_c", "v_c", "g_c")
    blk = {n: i for i, n in enumerate(names)}
    alpha = (2 * depth) ** 0.25
    lam_inits = tuple(0.8 - 0.6 * math.exp(-0.3 * l) for l in range(depth))

    mod_rows = -(-(1 + dec_batch) // 8) * 8
    cvec = jnp.zeros((mod_rows, D), F32).at[0].set(c_ctx).at[1:1 + dec_batch].set(c)
    mod = _ada_mod(cvec, w_ada, b_ada).reshape(depth * mod_rows, 1, 3, D)
    lam_vec = _diff_lambdas(lambda_q1, lambda_k1, lambda_q2, lambda_k2, lam_inits)

    w_in_b = w_in.astype(BF16)
    w_s_b = w_spatial.astype(BF16)
    b_s_full = jnp.repeat(jnp.swapaxes(b_spatial, 1, 2), SG_GROUP_W, axis=2)
    wmg_b, wa_b, wb_b, wc_b, wo_b = (w.astype(BF16) for w in (w_mgate, w_br_a, w_br_b, w_br_c, w_out))
    ck = cache_diff_k.transpose(0, 1, 4, 2, 3, 5).reshape(dec_batch, depth, past, d_b).astype(BF16)
    cv = cache_diff_v.transpose(0, 1, 3, 2, 4).reshape(dec_batch, depth, past, d_b).astype(BF16)
    nk = cache_na_k.transpose(0, 1, 3, 2, 4).reshape(dec_batch, depth, past, d_c).astype(BF16)
    nv = cache_na_v.transpose(0, 1, 3, 2, 4).reshape(dec_batch, depth, past, d_c).astype(BF16)
    rope_tabs = _rope_tables(dec_seq)
    bias_tab = _na_bias_tables(na_rel_bias)

    xp = x_prompt.reshape(1, batch * seq, D)
    xs = x_sample
    new_dk, new_dv, new_nk, new_nv = [], [], [], []
    for l in range(depth):
        row2 = lambda a: a[l].reshape(1, -1)
        merge_w = (wmg_b[l], row2(b_mgate), wa_b[l], wb_b[l], wc_b[l], wo_b[l], row2(ln_g), row2(ln_b))
        sg = (row2(sg_norm_g), row2(sg_norm_b), w_s_b[l], b_s_full[l])

        h, z = _inproj(xp, mod, l * mod_rows, w_in_b[l], blk, tn, F32)
        zc = z.reshape(batch, seq, d_in)
        ya = _mixer_a(zc, blk, *sg)
        yb = _diff_attn(zc, blk, d_b, lam_vec, row2(diff_subln_g), l, lam_inits[l])
        yc = _softmax_attn(zc, blk, d_c)
        flat = lambda a: a.reshape(1, batch * seq, a.shape[-1])
        xp = _merge(xp, h, flat(ya), flat(yb), flat(yc), mod, l * mod_rows, *merge_w, alpha)
        col = lambda n: zc[:, :, blk[n] * tn:(blk[n] + 1) * tn]
        new_dk.append(col("k_b").reshape(batch, seq, 2, h_b, HEAD_DIM).transpose(0, 2, 3, 1, 4))
        new_dv.append(col("v_b").reshape(batch, seq, h_b, 2 * HEAD_DIM).transpose(0, 2, 1, 3))
        new_nk.append(col("k_c").reshape(batch, seq, h_c, HEAD_DIM).transpose(0, 2, 1, 3))
        new_nv.append(col("v_c").reshape(batch, seq, h_c, HEAD_DIM).transpose(0, 2, 1, 3))

        h, z = _inproj(xs, mod, l * mod_rows + 1, w_in_b[l], blk, tn, BF16, rope_tabs)
        ya = _mixer_a(z, blk, *sg)
        yb = _diff_attn(z, blk, d_b, lam_vec, row2(diff_subln_g), l, lam_inits[l], cache=(ck, cv))
        yc = _na_attn(z, blk, d_c, nk, nv, bias_tab, l)
        xs = _merge(xs, h, ya, yb, yc, mod, l * mod_rows + 1, *merge_w, alpha)

    return (xp.reshape(batch, seq, D), xs,
            jnp.stack(new_dk, axis=1), jnp.stack(new_dv, axis=1),
            jnp.stack(new_nk, axis=1), jnp.stack(new_nv, axis=1))
```

```python
import functools
import math

import jax
import jax.numpy as jnp
import numpy as np
from jax import lax
from jax.experimental import pallas as pl
from jax.experimental.pallas import tpu as pltpu

GRID_W = 64
CHUNK = 128
HEAD_DIM = 64
SG_GROUP_W = 128
WIN_R = 8
WIN_W = 16
NA_ROWS_PER_BLOCK = 4
NA_BLOCK_UNROLL = 8
ROPE_THETA = 10000.0
EPS = 1e-6
NEG_INF = -1e30
LANES = 128
LOG2E = math.log2(math.e)
Q_SCALE = HEAD_DIM ** -0.5 * LOG2E

BF16 = jnp.bfloat16
F32 = jnp.float32

VMEM_LIMIT = 56 * 1024 * 1024


def _cparams(sem):
    return pltpu.CompilerParams(dimension_semantics=sem, vmem_limit_bytes=VMEM_LIMIT)


def _sigmoid(x):
    return 1.0 / (1.0 + jnp.exp(-x))


def _silu(x):
    return x * _sigmoid(x)


def _dot(a, b):
    return jnp.dot(a, b, preferred_element_type=F32)


def _dot_nt(a, b):
    return lax.dot_general(a, b, (((1,), (1,)), ((), ())), preferred_element_type=F32)


def _layernorm(x):
    mu = jnp.mean(x, axis=-1, keepdims=True)
    xc = x - mu
    var = jnp.mean(xc * xc, axis=-1, keepdims=True)
    return xc * lax.rsqrt(var + EPS)


def _fold_lanes(x, op):
    out = x[:, :LANES]
    for i in range(1, x.shape[1] // LANES):
        out = op(out, x[:, i * LANES:(i + 1) * LANES])
    return out


def _ada_kernel(c_ref, w_ref, b_ref, o_ref):
    s = _silu(c_ref[...]).astype(BF16)
    o_ref[0] = _dot(s, w_ref[0].astype(BF16)) + b_ref[0]


def _ada_mod(cvec, w_ada, b_ada):
    depth, d, d3 = w_ada.shape
    rows = cvec.shape[0]
    tn = 1024
    return pl.pallas_call(
        _ada_kernel,
        grid=(depth, d3 // tn),
        in_specs=[
            pl.BlockSpec((rows, d), lambda l, j: (0, 0)),
            pl.BlockSpec((1, d, tn), lambda l, j: (l, 0, j)),
            pl.BlockSpec((1, 1, tn), lambda l, j: (l, 0, j)),
        ],
        out_specs=pl.BlockSpec((1, rows, tn), lambda l, j: (l, 0, j)),
        out_shape=jax.ShapeDtypeStruct((depth, rows, d3), F32),
        compiler_params=_cparams(("parallel", "parallel")),
        name="ada_mod",
    )(cvec, w_ada, b_ada.reshape(depth, 1, d3))


def _lambda_kernel(lam_ref, o_ref, *, lam_inits):
    p = lam_ref[...]
    t1 = jnp.sum(p[:, 0, :] * p[:, 1, :], axis=-1, keepdims=True)
    t2 = jnp.sum(p[:, 2, :] * p[:, 3, :], axis=-1, keepdims=True)
    layer = lax.broadcasted_iota(jnp.int32, t1.shape, 0)
    init = jnp.zeros_like(t1)
    for l, v in enumerate(lam_inits):
        init = jnp.where(layer == l, v, init)
    o_ref[...] = jnp.broadcast_to(jnp.exp(t1) - jnp.exp(t2) + init, o_ref.shape)


def _diff_lambdas(lq1, lk1, lq2, lk2, lam_inits):
    depth = lq1.shape[0]
    params = jnp.stack([lq1, lk1, lq2, lk2], axis=1).astype(F32)
    out = pl.pallas_call(
        functools.partial(_lambda_kernel, lam_inits=lam_inits),
        out_shape=jax.ShapeDtypeStruct((depth, LANES), F32),
        name="diff_lambda",
    )(params)
    return out[:, 0]


def _rope(a, c, s_up, s_dn):
    outs = []
    for i in range(a.shape[1] // LANES):
        g = a[:, i * LANES:(i + 1) * LANES]
        outs.append(g * c + pltpu.roll(g, LANES - 16, 1) * s_up + pltpu.roll(g, 16, 1) * s_dn)
    return jnp.concatenate(outs, axis=1)


def _inproj_kernel(*refs, rope, blk, tn):
    if rope:
        x_ref, mod_ref, w_ref, c_ref, su_ref, sd_ref, h_ref, z_ref = refs
    else:
        x_ref, mod_ref, w_ref, h_ref, z_ref = refs
    xn = _layernorm(x_ref[0])
    hb = (xn * (1.0 + mod_ref[0, 0, 1:2, :]) + mod_ref[0, 0, 0:1, :]).astype(BF16)
    h_ref[0] = hb
    for j in range(w_ref.shape[1] // tn):
        cols = slice(j * tn, (j + 1) * tn)
        acc = _dot(hb, w_ref[:, cols])
        if rope and j in (blk["q_b"], blk["k_b"]):
            acc = _rope(acc, c_ref[...], su_ref[...], sd_ref[...])
        if j in (blk["q_b"], blk["q_c"]):
            acc = acc * Q_SCALE
        z_ref[0, :, cols] = acc.astype(z_ref.dtype)


def _inproj(x, mod, mod_row0, w, blk, tn, z_dtype, rope_tabs=None):
    B, L, D = x.shape
    d_in = w.shape[1]
    tm = min(L, 512 if z_dtype == BF16 else 256)
    rope = rope_tabs is not None
    in_specs = [
        pl.BlockSpec((1, tm, D), lambda b, i: (b, i, 0)),
        pl.BlockSpec((1, 1, 3, D), lambda b, i: (mod_row0 + b, 0, 0, 0)),
        pl.BlockSpec((D, d_in), lambda b, i: (0, 0), pipeline_mode=pl.Buffered(1)),
    ]
    args = [x, mod, w]
    if rope:
        in_specs += [pl.BlockSpec((tm, LANES), lambda b, i: (i, 0))] * 3
        args += list(rope_tabs)
    return pl.pallas_call(
        functools.partial(_inproj_kernel, rope=rope, blk=blk, tn=tn),
        grid=(B, L // tm),
        in_specs=in_specs,
        out_specs=[
            pl.BlockSpec((1, tm, D), lambda b, i: (b, i, 0)),
            pl.BlockSpec((1, tm, d_in), lambda b, i: (b, i, 0)),
        ],
        out_shape=[
            jax.ShapeDtypeStruct((B, L, D), BF16),
            jax.ShapeDtypeStruct((B, L, d_in), z_dtype),
        ],
        compiler_params=_cparams(("parallel", "parallel")),
        name="inproj_rope" if rope else "inproj",
    )(*args)


def _mixer_a_kernel(u_ref, v_ref, g_ref, ng_ref, nb_ref, ws_ref, bs_ref, o_ref):
    tm, d_a = v_ref.shape[1], v_ref.shape[2]
    n_groups = d_a // SG_GROUP_W
    vn = (_layernorm(v_ref[0].astype(F32)) * ng_ref[...] + nb_ref[...]).astype(BF16)
    for n in range(tm // CHUNK):
        rows = slice(n * CHUNK, (n + 1) * CHUNK)
        sv = jnp.concatenate(
            [_dot(ws_ref[g], vn[rows, g * SG_GROUP_W:(g + 1) * SG_GROUP_W]) for g in range(n_groups)],
            axis=1) + bs_ref[...]
        y = u_ref[0, rows, :].astype(F32) * sv * _silu(g_ref[0, rows, :].astype(F32))
        o_ref[0, rows, :] = y.astype(o_ref.dtype)


def _mixer_a(z, blk, norm_g, norm_b, w_s, b_s_full):
    B, L, _ = z.shape
    d_a = norm_g.shape[-1]
    tm = min(L, 512)
    zspec = lambda name: pl.BlockSpec((1, tm, d_a), lambda b, i, c=blk[name]: (b, i, c))
    const2 = lambda shp: pl.BlockSpec(shp, lambda b, i: (0, 0))
    return pl.pallas_call(
        _mixer_a_kernel,
        grid=(B, L // tm),
        in_specs=[
            zspec("u_a"), zspec("v_a"), zspec("g_a"),
            const2((1, d_a)), const2((1, d_a)),
            pl.BlockSpec(w_s.shape, lambda b, i: (0, 0, 0)),
            const2((CHUNK, d_a)),
        ],
        out_specs=pl.BlockSpec((1, tm, d_a), lambda b, i: (b, i, 0)),
        out_shape=jax.ShapeDtypeStruct((B, L, d_a), BF16),
        compiler_params=_cparams(("parallel", "parallel")),
        name="mixer_a",
    )(z, z, z, norm_g, norm_b, w_s, b_s_full)


def _diff_attn_kernel(*refs, layer, lam_init, tk, n_self, n_ctx):
    if n_ctx:
        (lam_ref, q1_ref, q2_ref, k1_ref, k2_ref, v_ref, g_ref, kc1_ref, kc2_ref, vc_ref,
         sg_ref, o_ref) = refs
    else:
        lam_ref, q1_ref, q2_ref, k1_ref, k2_ref, v_ref, g_ref, sg_ref, o_ref = refs
    lam = lam_ref[layer]
    tq = q1_ref.shape[1]
    dv = 2 * HEAD_DIM
    lane = lax.broadcasted_iota(jnp.int32, (1, LANES), 1)
    q_pair = (q1_ref[0], q2_ref[0])
    ones = jnp.ones((tk, LANES), BF16)

    chunks = [(k1_ref.at[0, c * tk:(c + 1) * tk, :], k2_ref.at[0, c * tk:(c + 1) * tk, :],
               v_ref.at[0, c * tk:(c + 1) * tk, :]) for c in range(n_self)]
    if n_ctx:
        chunks += [(kc1_ref.at[0, 0, c * tk:(c + 1) * tk, :], kc2_ref.at[0, 0, c * tk:(c + 1) * tk, :],
                    vc_ref.at[0, 0, c * tk:(c + 1) * tk, :]) for c in range(n_ctx)]

    for hh in range(2):
        head_lanes = (lane < HEAD_DIM) if hh == 0 else (lane >= HEAD_DIM)
        qm = [jnp.where(head_lanes, q, jnp.zeros_like(q)).astype(BF16) for q in q_pair]
        state = [(jnp.full((tq, 1), -jnp.inf, F32), jnp.zeros((tq, 2 * dv), F32)) for _ in range(2)]
        for k1c, k2c, vc in chunks:
            vaug = jnp.concatenate([vc[:, hh * dv:(hh + 1) * dv].astype(BF16), ones], axis=1)
            for m, kc in enumerate((k1c, k2c)):
                mx, acc = state[m]
                s = _dot_nt(qm[m], kc[...].astype(BF16))
                mx_new = jnp.maximum(mx, jnp.max(s, axis=-1, keepdims=True))
                p = jnp.exp2(s - mx_new).astype(BF16)
                state[m] = (mx_new, jnp.exp2(mx - mx_new) * acc + _dot(p, vaug))

        (_, a1), (_, a2) = state
        o = a1[:, :dv] * (1.0 / a1[:, dv:dv + 1]) - a2[:, :dv] * (lam / a2[:, dv:dv + 1])
        of = o * lax.rsqrt(jnp.mean(o * o, axis=-1, keepdims=True) + EPS)
        of = of * sg_ref[...] * (1.0 - lam_init)
        y = of * _silu(g_ref[0, :, hh * dv:(hh + 1) * dv].astype(F32))
        o_ref[0, :, hh * dv:(hh + 1) * dv] = y.astype(o_ref.dtype)


def _diff_attn(z, blk, d_b, lam_vec, subln_g, layer, lam_init, cache=None):
    B, L, _ = z.shape
    tq = min(L, 256)
    tk = min(L, 512)
    n_pairs = d_b // (4 * HEAD_DIM)
    col128 = lambda name: blk[name] * (d_b // LANES)
    col256 = lambda name: blk[name] * (d_b // (2 * LANES))
    qspec = lambda m: pl.BlockSpec((1, tq, LANES), lambda b, p, i: (b, i, col128("q_b") + m * n_pairs + p))
    kspec = lambda m: pl.BlockSpec((1, L, LANES), lambda b, p, i: (b, 0, col128("k_b") + m * n_pairs + p))
    in_specs = [
        pl.BlockSpec(memory_space=pltpu.SMEM),
        qspec(0), qspec(1), kspec(0), kspec(1),
        pl.BlockSpec((1, L, 2 * LANES), lambda b, p, i: (b, 0, col256("v_b") + p)),
        pl.BlockSpec((1, tq, 2 * LANES), lambda b, p, i: (b, i, col256("g_b") + p)),
    ]
    args = [lam_vec, z, z, z, z, z, z]
    n_ctx = 0
    if cache is not None:
        ck, cv = cache
        P = ck.shape[2]
        assert P % tk == 0
        n_ctx = P // tk
        in_specs += [
            pl.BlockSpec((1, 1, P, LANES), lambda b, p, i: (b, layer, 0, p)),
            pl.BlockSpec((1, 1, P, LANES), lambda b, p, i: (b, layer, 0, n_pairs + p)),
            pl.BlockSpec((1, 1, P, 2 * LANES), lambda b, p, i: (b, layer, 0, p)),
        ]
        args += [ck, ck, cv]
    in_specs.append(pl.BlockSpec((1, 2 * HEAD_DIM), lambda b, p, i: (0, 0)))
    args.append(subln_g)
    n_self = L // tk
    return pl.pallas_call(
        functools.partial(_diff_attn_kernel, layer=layer, lam_init=lam_init, tk=tk, n_self=n_self, n_ctx=n_ctx),
        grid=(B, n_pairs, L // tq),
        in_specs=in_specs,
        out_specs=pl.BlockSpec((1, tq, 2 * LANES), lambda b, p, i: (b, i, p)),
        out_shape=jax.ShapeDtypeStruct((B, L, d_b), BF16),
        compiler_params=_cparams(("parallel", "parallel", "parallel")),
        name="diff_attn_latent" if cache is not None else "diff_attn_ctx",
    )(*args)


def _softmax_attn_kernel(q_ref, k_ref, v_ref, g_ref, o_ref):
    lane = lax.broadcasted_iota(jnp.int32, (1, LANES), 1)
    q = q_ref[0]
    k = k_ref[0].astype(BF16)
    v = v_ref[0].astype(BF16)
    outs = []
    for hh in range(2):
        head_lanes = (lane < HEAD_DIM) if hh == 0 else (lane >= HEAD_DIM)
        qm = jnp.where(head_lanes, q, jnp.zeros_like(q)).astype(BF16)
        s = _dot_nt(qm, k)
        e = jnp.exp2(s - jnp.max(s, axis=-1, keepdims=True))
        o = _dot(e.astype(BF16), v) * (1.0 / jnp.sum(e, axis=-1, keepdims=True))
        outs.append(o)
    o = jnp.where(lane < HEAD_DIM, outs[0], outs[1])
    o_ref[0] = (o * _silu(g_ref[0].astype(F32))).astype(o_ref.dtype)


def _softmax_attn(z, blk, d_c):
    B, L, _ = z.shape
    n_pairs = d_c // LANES
    spec = lambda name: pl.BlockSpec((1, L, LANES), lambda b, p, c=blk[name] * n_pairs: (b, 0, c + p))
    return pl.pallas_call(
        _softmax_attn_kernel,
        grid=(B, n_pairs),
        in_specs=[spec("q_c"), spec("k_c"), spec("v_c"), spec("g_c")],
        out_specs=pl.BlockSpec((1, L, LANES), lambda b, p: (b, 0, p)),
        out_shape=jax.ShapeDtypeStruct((B, L, d_c), BF16),
        compiler_params=_cparams(("parallel", "parallel")),
        name="softmax_attn_ctx",
    )(z, z, z, z)


def _na_kernel(q_ref, k_ref, v_ref, g_ref, kc_ref, vc_ref, bias_ref, o_ref, *, rows):
    lane = lax.broadcasted_iota(jnp.int32, (1, LANES), 1)
    n_loc = WIN_R * GRID_W
    P = kc_ref.shape[2]
    kc = kc_ref[0, 0]
    vc_aug = jnp.concatenate([vc_ref[0, 0], jnp.ones((P, LANES), BF16)], axis=1)
    ones_loc = jnp.ones((n_loc, LANES), BF16)
    tq = NA_ROWS_PER_BLOCK * GRID_W
    lo = lane < HEAD_DIM

    def block(i, carry):
        q0 = pl.multiple_of(i * tq, tq)
        q = q_ref[0, pl.ds(q0, tq), :]
        zero = jnp.zeros_like(q)
        q_heads = (jnp.where(lo, q, zero), jnp.where(lo, zero, q))
        q2 = jnp.concatenate([qh[j * GRID_W:(j + 1) * GRID_W] for j in range(NA_ROWS_PER_BLOCK) for qh in q_heads],
                             axis=0)
        s_ctx = _dot_nt(q2, kc)
        m_ctx = jnp.max(s_ctx, axis=-1, keepdims=True)
        o_rows, p_ctx_rows = [], []
        for j in range(NA_ROWS_PER_BLOCK):
            r = i * NA_ROWS_PER_BLOCK + j
            rs = jnp.clip(r - WIN_R // 2, 0, rows - WIN_R)
            off = rs - r + WIN_R - 1
            k0 = pl.multiple_of(rs * GRID_W, GRID_W)
            kl = k_ref[0, pl.ds(k0, n_loc), :]
            vl_aug = jnp.concatenate([v_ref[0, pl.ds(k0, n_loc), :], ones_loc], axis=1)
            rsl = slice(2 * j * GRID_W, 2 * (j + 1) * GRID_W)
            s_loc = _dot_nt(q2[rsl], kl) + bias_ref[0, :, off].reshape(2 * GRID_W, n_loc)
            m = jnp.maximum(jnp.max(s_loc, axis=-1, keepdims=True), m_ctx[rsl])
            p_ctx_rows.append(jnp.exp2(s_ctx[rsl] - m).astype(BF16))
            o_rows.append(_dot(jnp.exp2(s_loc - m).astype(BF16), vl_aug))
        o = jnp.concatenate(o_rows, axis=0) + _dot(jnp.concatenate(p_ctx_rows, axis=0), vc_aug)
        o = o[:, :LANES] * (1.0 / o[:, LANES:LANES + 1])
        o = jnp.concatenate(
            [jnp.where(lo, o[2 * j * GRID_W:(2 * j + 1) * GRID_W], o[(2 * j + 1) * GRID_W:(2 * j + 2) * GRID_W])
             for j in range(NA_ROWS_PER_BLOCK)], axis=0)
        y = o * _silu(g_ref[0, pl.ds(q0, tq), :].astype(F32))
        o_ref[0, pl.ds(q0, tq), :] = y.astype(o_ref.dtype)
        return carry

    n_blocks = rows // NA_ROWS_PER_BLOCK
    lax.fori_loop(0, n_blocks, block, 0, unroll=math.gcd(n_blocks, NA_BLOCK_UNROLL))


def _na_attn(z, blk, d_c, nk, nv, bias_tab, layer):
    B, L, _ = z.shape
    rows = L // GRID_W
    assert rows >= WIN_R and L % GRID_W == 0 and rows % NA_ROWS_PER_BLOCK == 0
    P = nk.shape[2]
    n_pairs = d_c // LANES
    spec = lambda name: pl.BlockSpec((1, L, LANES), lambda b, p, c=blk[name] * n_pairs: (b, 0, c + p))
    cspec = pl.BlockSpec((1, 1, P, LANES), lambda b, p: (b, layer, 0, p))
    return pl.pallas_call(
        functools.partial(_na_kernel, rows=rows),
        grid=(B, n_pairs),
        in_specs=[
            spec("q_c"), spec("k_c"), spec("v_c"), spec("g_c"), cspec, cspec,
            pl.BlockSpec((1, 2, WIN_R, GRID_W, WIN_R * GRID_W), lambda b, p: (layer, p, 0, 0, 0)),
        ],
        out_specs=pl.BlockSpec((1, L, LANES), lambda b, p: (b, 0, p)),
        out_shape=jax.ShapeDtypeStruct((B, L, d_c), BF16),
        compiler_params=_cparams(("parallel", "parallel")),
        name="na_attn_latent",
    )(z, z, z, z, nk, nv, bias_tab)


def _merge_kernel(x_ref, h_ref, ya_ref, yb_ref, yc_ref, mod_ref, wmg_ref, bmg_ref, wa_ref, wb_ref, wc_ref,
                  wo_ref, lg_ref, lb_ref, o_ref, *, alpha):
    D = x_ref.shape[2]
    h = h_ref[0]
    m = None
    for i, (y_ref, w_ref) in enumerate(((ya_ref, wa_ref), (yb_ref, wb_ref), (yc_ref, wc_ref))):
        gate = _sigmoid(_dot(h, wmg_ref[:, i * D:(i + 1) * D]) + bmg_ref[:, i * D:(i + 1) * D])
        t = gate * _dot(y_ref[0], w_ref[...])
        m = t if m is None else m + t
    out = _dot(m.astype(BF16), wo_ref[...])
    t = alpha * x_ref[0] + mod_ref[0, 0, 2:3, :] * out
    o_ref[0] = _layernorm(t) * lg_ref[...] + lb_ref[...]


def _merge(x, h, ya, yb, yc, mod, mod_row0, wmg, bmg, wa, wb, wc, wo, lg, lb, alpha):
    B, L, D = x.shape
    tm = min(L, 512)
    tok = lambda w: pl.BlockSpec((1, tm, w), lambda b, i: (b, i, 0))
    const = lambda a: pl.BlockSpec(a.shape, lambda b, i: (0,) * a.ndim, pipeline_mode=pl.Buffered(1))
    return pl.pallas_call(
        functools.partial(_merge_kernel, alpha=alpha),
        grid=(B, L // tm),
        in_specs=[
            tok(D), tok(D), tok(ya.shape[2]), tok(yb.shape[2]), tok(yc.shape[2]),
            pl.BlockSpec((1, 1, 3, D), lambda b, i: (mod_row0 + b, 0, 0, 0)),
            const(wmg), const(bmg), const(wa), const(wb), const(wc), const(wo), const(lg), const(lb),
        ],
        out_specs=tok(D),
        out_shape=jax.ShapeDtypeStruct((B, L, D), F32),
        compiler_params=_cparams(("parallel", "parallel")),
        name="merge_postnorm",
    )(x, h, ya, yb, yc, mod, wmg, bmg, wa, wb, wc, wo, lg, lb)


def _rope_tables(n_tok):
    half = HEAD_DIM // 2
    quarter = HEAD_DIM // 4
    t = jnp.arange(n_tok)
    inv = 1.0 / (ROPE_THETA ** (jnp.arange(0, half, 2, dtype=F32) / half))
    ang = jnp.stack([(t // GRID_W).astype(F32)[:, None] * inv, (t % GRID_W).astype(F32)[:, None] * inv], axis=1)
    cos = jnp.cos(ang)[:, :, None, :]
    sin = jnp.sin(ang)[:, :, None, :]
    zero = jnp.zeros_like(sin)
    shape = (n_tok, 2, 2, quarter)
    c = jnp.broadcast_to(cos, shape).reshape(n_tok, HEAD_DIM)
    s_up = jnp.concatenate([-sin, zero], axis=2).reshape(n_tok, HEAD_DIM)
    s_dn = jnp.concatenate([zero, sin], axis=2).reshape(n_tok, HEAD_DIM)
    rep = LANES // HEAD_DIM
    return tuple(jnp.tile(a, (1, rep)) for a in (c, s_up, s_dn))


def _na_bias_kernel(rb_ref, o_ref):
    shape = (GRID_W, LANES)
    qcol = lax.broadcasted_iota(jnp.int32, shape, 0)
    lane = lax.broadcasted_iota(jnp.int32, shape, 1)
    kcol = lane & (GRID_W - 1)
    cstart = jnp.clip(qcol - WIN_W // 2, 0, GRID_W - WIN_W)
    in_window = (kcol >= cstart) & (kcol < cstart + WIN_W)

    def toeplitz(y, lane0):
        row = jnp.broadcast_to(rb_ref[0, y:y + 1, :], shape)
        return pltpu.roll(row, (lane0 - (WIN_W - 1)) % LANES, 1, stride=1, stride_axis=0)

    n_dy = 2 * WIN_R - 1
    t_lo = [toeplitz(y, 0) for y in range(n_dy)]
    t_hi = [toeplitz(y, GRID_W) for y in range(n_dy)]
    for off in range(WIN_R):
        for jp in range(WIN_R // 2):
            y = off + 2 * jp
            tile = jnp.where(lane < GRID_W, t_lo[y], t_hi[y + 1]) * LOG2E
            o_ref[0, 0, off, :, jp * LANES:(jp + 1) * LANES] = jnp.where(in_window, tile, NEG_INF)


def _na_bias_tables(rel_bias):
    depth, n_heads, n_dy, n_dx = rel_bias.shape
    assert n_dy == 2 * WIN_R - 1 and n_dx == 2 * WIN_W - 1 and WIN_R % 2 == 0
    rb = jnp.pad(rel_bias.astype(F32), ((0, 0), (0, 0), (0, 16 - n_dy), (0, LANES - n_dx)))
    return pl.pallas_call(
        _na_bias_kernel,
        grid=(depth, n_heads),
        in_specs=[pl.BlockSpec((1, 16, LANES), lambda l, h: (l * n_heads + h, 0, 0))],
        out_specs=pl.BlockSpec((1, 1, WIN_R, GRID_W, WIN_R * GRID_W), lambda l, h: (l, h, 0, 0, 0)),
        out_shape=jax.ShapeDtypeStruct((depth, n_heads, WIN_R, GRID_W, WIN_R * GRID_W), F32),
        compiler_params=_cparams(("parallel", "parallel")),
        name="na_bias_table",
    )(rb.reshape(depth * n_heads, 16, LANES))


def kernel(x_prompt, x_sample, c, cache_diff_k, cache_diff_v, cache_na_k, cache_na_v, c_ctx, w_ada, b_ada, w_in, sg_norm_g, sg_norm_b, w_spatial, b_spatial, lambda_q1, lambda_k1, lambda_q2, lambda_k2, diff_subln_g, na_rel_bias, w_br_a, w_br_b, w_br_c, w_mgate, b_mgate, w_out, ln_g, ln_b):
    depth, D, d_in = w_in.shape
    batch, seq, _ = x_prompt.shape
    dec_batch, dec_seq, _ = x_sample.shape
    past = cache_diff_k.shape[4]
    d_a, d_b, d_c = w_br_a.shape[1], w_br_b.shape[1], w_br_c.shape[1]
    h_b = d_b // (2 * HEAD_DIM)
    h_c = d_c // HEAD_DIM
    tn = d_a
    assert d_a == d_b == d_c and d_in == 11 * tn and tn % (2 * LANES) == 0
    assert d_a // SG_GROUP_W == w_spatial.shape[1] and w_spatial.shape[2] == CHUNK
    names = ("u_a", "v_a", "g_a", "q_b", "k_b", "v_b", "g_b", "q_c", "k_c", "v_c", "g_c")
    blk = {n: i for i, n in enumerate(names)}
    alpha = (2 * depth) ** 0.25
    lam_inits = tuple(0.8 - 0.6 * math.exp(-0.3 * l) for l in range(depth))

    mod_rows = -(-(1 + dec_batch) // 8) * 8
    cvec = jnp.zeros((mod_rows, D), F32).at[0].set(c_ctx).at[1:1 + dec_batch].set(c)
    mod = _ada_mod(cvec, w_ada, b_ada).reshape(depth * mod_rows, 1, 3, D)
    lam_vec = _diff_lambdas(lambda_q1, lambda_k1, lambda_q2, lambda_k2, lam_inits)

    w_in_b = w_in.astype(BF16)
    w_s_b = w_spatial.astype(BF16)
    b_s_full = jnp.repeat(jnp.swapaxes(b_spatial, 1, 2), SG_GROUP_W, axis=2)
    wmg_b, wa_b, wb_b, wc_b, wo_b = (w.astype(BF16) for w in (w_mgate, w_br_a, w_br_b, w_br_c, w_out))
    ck = cache_diff_k.transpose(0, 1, 4, 2, 3, 5).reshape(dec_batch, depth, past, d_b).astype(BF16)
    cv = cache_diff_v.transpose(0, 1, 3, 2, 4).reshape(dec_batch, depth, past, d_b).astype(BF16)
    nk = cache_na_k.transpose(0, 1, 3, 2, 4).reshape(dec_batch, depth, past, d_c).astype(BF16)
    nv = cache_na_v.transpose(0, 1, 3, 2, 4).reshape(dec_batch, depth, past, d_c).astype(BF16)
    rope_tabs = _rope_tables(dec_seq)
    bias_tab = _na_bias_tables(na_rel_bias)

    xp = x_prompt.reshape(1, batch * seq, D)
    xs = x_sample
    new_dk, new_dv, new_nk, new_nv = [], [], [], []
    for l in range(depth):
        row2 = lambda a: a[l].reshape(1, -1)
        merge_w = (wmg_b[l], row2(b_mgate), wa_b[l], wb_b[l], wc_b[l], wo_b[l], row2(ln_g), row2(ln_b))
        sg = (row2(sg_norm_g), row2(sg_norm_b), w_s_b[l], b_s_full[l])

        h, z = _inproj(xp, mod, l * mod_rows, w_in_b[l], blk, tn, F32)
        zc = z.reshape(batch, seq, d_in)
        ya = _mixer_a(zc, blk, *sg)
        yb = _diff_attn(zc, blk, d_b, lam_vec, row2(diff_subln_g), l, lam_inits[l])
        yc = _softmax_attn(zc, blk, d_c)
        flat = lambda a: a.reshape(1, batch * seq, a.shape[-1])
        xp = _merge(xp, h, flat(ya), flat(yb), flat(yc), mod, l * mod_rows, *merge_w, alpha)
        col = lambda n: zc[:, :, blk[n] * tn:(blk[n] + 1) * tn]
        new_dk.append(col("k_b").reshape(batch, seq, 2, h_b, HEAD_DIM).transpose(0, 2, 3, 1, 4))
        new_dv.append(col("v_b").reshape(batch, seq, h_b, 2 * HEAD_DIM).transpose(0, 2, 1, 3))
        new_nk.append(col("k_c").reshape(batch, seq, h_c, HEAD_DIM).transpose(0, 2, 1, 3))
        new_nv.append(col("v_c").reshape(batch, seq, h_c, HEAD_DIM).transpose(0, 2, 1, 3))

        h, z = _inproj(xs, mod, l * mod_rows + 1, w_in_b[l], blk, tn, BF16, rope_tabs)
        ya = _mixer_a(z, blk, *sg)
        yb = _diff_attn(z, blk, d_b, lam_vec, row2(diff_subln_g), l, lam_inits[l], cache=(ck, cv))
        yc = _na_attn(z, blk, d_c, nk, nv, bias_tab, l)
        xs = _merge(xs, h, ya, yb, yc, mod, l * mod_rows + 1, *merge_w, alpha)

    return (xp.reshape(batch, seq, D), xs,
            jnp.stack(new_dk, axis=1), jnp.stack(new_dv, axis=1),
            jnp.stack(new_nk, axis=1), jnp.stack(new_nv, axis=1))
```

```python
import functools
import math

import jax
import jax.numpy as jnp
import numpy as np
from jax import lax
from jax.experimental import pallas as pl
from jax.experimental.pallas import tpu as pltpu

GRID_W = 64
CHUNK = 128
HEAD_DIM = 64
SG_GROUP_W = 128
WIN_R = 8
WIN_W = 16
DA_TQ = 256
DA_TK = 512
DA_SUBTILES = 2
NA_ROWS_PER_BLOCK = 4
NA_BLOCK_UNROLL = 8
ROPE_THETA = 10000.0
EPS = 1e-6
NEG_INF = -1e30
LANES = 128
LOG2E = math.log2(math.e)
Q_SCALE = HEAD_DIM ** -0.5 * LOG2E

BF16 = jnp.bfloat16
F32 = jnp.float32

VMEM_LIMIT = 56 * 1024 * 1024


def _cparams(sem):
    return pltpu.CompilerParams(dimension_semantics=sem, vmem_limit_bytes=VMEM_LIMIT)


def _sigmoid(x):
    return 1.0 / (1.0 + jnp.exp(-x))


def _silu(x):
    return x * _sigmoid(x)


def _dot(a, b):
    return jnp.dot(a, b, preferred_element_type=F32)


def _dot_nt(a, b):
    return lax.dot_general(a, b, (((1,), (1,)), ((), ())), preferred_element_type=F32)


def _layernorm(x):
    mu = jnp.mean(x, axis=-1, keepdims=True)
    xc = x - mu
    var = jnp.mean(xc * xc, axis=-1, keepdims=True)
    return xc * lax.rsqrt(var + EPS)


def _fold_lanes(x, op):
    out = x[:, :LANES]
    for i in range(1, x.shape[1] // LANES):
        out = op(out, x[:, i * LANES:(i + 1) * LANES])
    return out


def _ada_kernel(c_ref, w_ref, b_ref, o_ref):
    s = _silu(c_ref[...]).astype(BF16)
    o_ref[0] = _dot(s, w_ref[0].astype(BF16)) + b_ref[0]


def _ada_mod(cvec, w_ada, b_ada):
    depth, d, d3 = w_ada.shape
    rows = cvec.shape[0]
    tn = 1024
    return pl.pallas_call(
        _ada_kernel,
        grid=(depth, d3 // tn),
        in_specs=[
            pl.BlockSpec((rows, d), lambda l, j: (0, 0)),
            pl.BlockSpec((1, d, tn), lambda l, j: (l, 0, j)),
            pl.BlockSpec((1, 1, tn), lambda l, j: (l, 0, j)),
        ],
        out_specs=pl.BlockSpec((1, rows, tn), lambda l, j: (l, 0, j)),
        out_shape=jax.ShapeDtypeStruct((depth, rows, d3), F32),
        compiler_params=_cparams(("parallel", "parallel")),
        name="ada_mod",
    )(cvec, w_ada, b_ada.reshape(depth, 1, d3))


def _lambda_kernel(lam_ref, o_ref, *, lam_inits):
    p = lam_ref[...]
    t1 = jnp.sum(p[:, 0, :] * p[:, 1, :], axis=-1, keepdims=True)
    t2 = jnp.sum(p[:, 2, :] * p[:, 3, :], axis=-1, keepdims=True)
    layer = lax.broadcasted_iota(jnp.int32, t1.shape, 0)
    init = jnp.zeros_like(t1)
    for l, v in enumerate(lam_inits):
        init = jnp.where(layer == l, v, init)
    o_ref[...] = jnp.broadcast_to(jnp.exp(t1) - jnp.exp(t2) + init, o_ref.shape)


def _diff_lambdas(lq1, lk1, lq2, lk2, lam_inits):
    depth = lq1.shape[0]
    params = jnp.stack([lq1, lk1, lq2, lk2], axis=1).astype(F32)
    out = pl.pallas_call(
        functools.partial(_lambda_kernel, lam_inits=lam_inits),
        out_shape=jax.ShapeDtypeStruct((depth, LANES), F32),
        name="diff_lambda",
    )(params)
    return out[:, 0]


def _rope(a, c, s_up, s_dn):
    outs = []
    for i in range(a.shape[1] // LANES):
        g = a[:, i * LANES:(i + 1) * LANES]
        outs.append(g * c + pltpu.roll(g, LANES - 16, 1) * s_up + pltpu.roll(g, 16, 1) * s_dn)
    return jnp.concatenate(outs, axis=1)


def _inproj_kernel(*refs, rope, blk, tn):
    if rope:
        x_ref, mod_ref, w_ref, c_ref, su_ref, sd_ref, h_ref, z_ref = refs
    else:
        x_ref, mod_ref, w_ref, h_ref, z_ref = refs
    xn = _layernorm(x_ref[0])
    hb = (xn * (1.0 + mod_ref[0, 0, 1:2, :]) + mod_ref[0, 0, 0:1, :]).astype(BF16)
    h_ref[0] = hb
    for j in range(w_ref.shape[1] // tn):
        cols = slice(j * tn, (j + 1) * tn)
        acc = _dot(hb, w_ref[:, cols])
        if rope and j in (blk["q_b"], blk["k_b"]):
            acc = _rope(acc, c_ref[...], su_ref[...], sd_ref[...])
        if j in (blk["q_b"], blk["q_c"]):
            acc = acc * Q_SCALE
        z_ref[0, :, cols] = acc.astype(z_ref.dtype)


def _inproj(x, mod, mod_row0, w, blk, tn, z_dtype, rope_tabs=None):
    B, L, D = x.shape
    d_in = w.shape[1]
    tm = min(L, 512 if z_dtype == BF16 else 256)
    rope = rope_tabs is not None
    in_specs = [
        pl.BlockSpec((1, tm, D), lambda b, i: (b, i, 0)),
        pl.BlockSpec((1, 1, 3, D), lambda b, i: (mod_row0 + b, 0, 0, 0)),
        pl.BlockSpec((D, d_in), lambda b, i: (0, 0), pipeline_mode=pl.Buffered(1)),
    ]
    args = [x, mod, w]
    if rope:
        in_specs += [pl.BlockSpec((tm, LANES), lambda b, i: (i, 0))] * 3
        args += list(rope_tabs)
    return pl.pallas_call(
        functools.partial(_inproj_kernel, rope=rope, blk=blk, tn=tn),
        grid=(B, L // tm),
        in_specs=in_specs,
        out_specs=[
            pl.BlockSpec((1, tm, D), lambda b, i: (b, i, 0)),
            pl.BlockSpec((1, tm, d_in), lambda b, i: (b, i, 0)),
        ],
        out_shape=[
            jax.ShapeDtypeStruct((B, L, D), BF16),
            jax.ShapeDtypeStruct((B, L, d_in), z_dtype),
        ],
        compiler_params=_cparams(("parallel", "parallel")),
        name="inproj_rope" if rope else "inproj",
    )(*args)


def _diff_attn_kernel(*refs, layer, lam_init, tq, tk, n_self, n_ctx):
    if n_ctx:
        (lam_ref, q1_ref, q2_ref, k1_ref, k2_ref, v_ref, g_ref, kc1_ref, kc2_ref, vc_ref,
         sg_ref, o_ref) = refs
    else:
        lam_ref, q1_ref, q2_ref, k1_ref, k2_ref, v_ref, g_ref, sg_ref, o_ref = refs
    lam = lam_ref[layer]
    dv = 2 * HEAD_DIM
    lane = lax.broadcasted_iota(jnp.int32, (1, LANES), 1)
    ones = jnp.ones((tk, LANES), BF16)

    chunks = [(k1_ref.at[0, c * tk:(c + 1) * tk, :], k2_ref.at[0, c * tk:(c + 1) * tk, :],
               v_ref.at[0, c * tk:(c + 1) * tk, :]) for c in range(n_self)]
    if n_ctx:
        chunks += [(kc1_ref.at[0, 0, c * tk:(c + 1) * tk, :], kc2_ref.at[0, 0, c * tk:(c + 1) * tk, :],
                    vc_ref.at[0, 0, c * tk:(c + 1) * tk, :]) for c in range(n_ctx)]

    for sub in range(q1_ref.shape[1] // tq):
        rows = slice(sub * tq, (sub + 1) * tq)
        q_pair = (q1_ref[0, rows, :], q2_ref[0, rows, :])
        for hh in range(2):
            head_lanes = (lane < HEAD_DIM) if hh == 0 else (lane >= HEAD_DIM)
            qm = [jnp.where(head_lanes, q, jnp.zeros_like(q)).astype(BF16) for q in q_pair]
            state = [(jnp.full((tq, 1), -jnp.inf, F32), jnp.zeros((tq, 2 * dv), F32)) for _ in range(2)]
            for k1c, k2c, vc in chunks:
                vaug = jnp.concatenate([vc[:, hh * dv:(hh + 1) * dv].astype(BF16), ones], axis=1)
                for m, kc in enumerate((k1c, k2c)):
                    mx, acc = state[m]
                    s = _dot_nt(qm[m], kc[...].astype(BF16))
                    mx_new = jnp.maximum(mx, jnp.max(s, axis=-1, keepdims=True))
                    p = jnp.exp2(s - mx_new).astype(BF16)
                    state[m] = (mx_new, jnp.exp2(mx - mx_new) * acc + _dot(p, vaug))

            (_, a1), (_, a2) = state
            o = a1[:, :dv] * (1.0 / a1[:, dv:dv + 1]) - a2[:, :dv] * (lam / a2[:, dv:dv + 1])
            of = o * lax.rsqrt(jnp.mean(o * o, axis=-1, keepdims=True) + EPS)
            of = of * sg_ref[...] * (1.0 - lam_init)
            y = of * _silu(g_ref[0, rows, hh * dv:(hh + 1) * dv].astype(F32))
            o_ref[0, rows, hh * dv:(hh + 1) * dv] = y.astype(o_ref.dtype)


def _diff_attn(z, blk, d_b, lam_vec, subln_g, layer, lam_init, cache=None):
    B, L, _ = z.shape
    tq = min(L, DA_TQ)
    bq = min(L, DA_SUBTILES * tq)
    tk = min(L, DA_TK)
    n_pairs = d_b // (4 * HEAD_DIM)
    col128 = lambda name: blk[name] * (d_b // LANES)
    col256 = lambda name: blk[name] * (d_b // (2 * LANES))
    qspec = lambda m: pl.BlockSpec((1, bq, LANES), lambda b, p, i: (b, i, col128("q_b") + m * n_pairs + p))
    kspec = lambda m: pl.BlockSpec((1, L, LANES), lambda b, p, i: (b, 0, col128("k_b") + m * n_pairs + p))
    in_specs = [
        pl.BlockSpec(memory_space=pltpu.SMEM),
        qspec(0), qspec(1), kspec(0), kspec(1),
        pl.BlockSpec((1, L, 2 * LANES), lambda b, p, i: (b, 0, col256("v_b") + p)),
        pl.BlockSpec((1, bq, 2 * LANES), lambda b, p, i: (b, i, col256("g_b") + p)),
    ]
    args = [lam_vec, z, z, z, z, z, z]
    n_ctx = 0
    if cache is not None:
        ck, cv = cache
        P = ck.shape[2]
        assert P % tk == 0
        n_ctx = P // tk
        in_specs += [
            pl.BlockSpec((1, 1, P, LANES), lambda b, p, i: (b, layer, 0, p)),
            pl.BlockSpec((1, 1, P, LANES), lambda b, p, i: (b, layer, 0, n_pairs + p)),
            pl.BlockSpec((1, 1, P, 2 * LANES), lambda b, p, i: (b, layer, 0, p)),
        ]
        args += [ck, ck, cv]
    in_specs.append(pl.BlockSpec((1, 2 * HEAD_DIM), lambda b, p, i: (0, 0)))
    args.append(subln_g)
    n_self = L // tk
    return pl.pallas_call(
        functools.partial(_diff_attn_kernel, layer=layer, lam_init=lam_init, tq=tq, tk=tk, n_self=n_self,
                          n_ctx=n_ctx),
        grid=(B, n_pairs, L // bq),
        in_specs=in_specs,
        out_specs=pl.BlockSpec((1, bq, 2 * LANES), lambda b, p, i: (b, i, p)),
        out_shape=jax.ShapeDtypeStruct((B, L, d_b), BF16),
        compiler_params=_cparams(("parallel", "parallel", "parallel")),
        name="diff_attn_latent" if cache is not None else "diff_attn_ctx",
    )(*args)


def _softmax_attn_kernel(q_ref, k_ref, v_ref, g_ref, o_ref):
    lane = lax.broadcasted_iota(jnp.int32, (1, LANES), 1)
    q = q_ref[0]
    k = k_ref[0].astype(BF16)
    v = v_ref[0].astype(BF16)
    outs = []
    for hh in range(2):
        head_lanes = (lane < HEAD_DIM) if hh == 0 else (lane >= HEAD_DIM)
        qm = jnp.where(head_lanes, q, jnp.zeros_like(q)).astype(BF16)
        s = _dot_nt(qm, k)
        e = jnp.exp2(s - jnp.max(s, axis=-1, keepdims=True))
        o = _dot(e.astype(BF16), v) * (1.0 / jnp.sum(e, axis=-1, keepdims=True))
        outs.append(o)
    o = jnp.where(lane < HEAD_DIM, outs[0], outs[1])
    o_ref[0] = (o * _silu(g_ref[0].astype(F32))).astype(o_ref.dtype)


def _softmax_attn(z, blk, d_c):
    B, L, _ = z.shape
    n_pairs = d_c // LANES
    spec = lambda name: pl.BlockSpec((1, L, LANES), lambda b, p, c=blk[name] * n_pairs: (b, 0, c + p))
    return pl.pallas_call(
        _softmax_attn_kernel,
        grid=(B, n_pairs),
        in_specs=[spec("q_c"), spec("k_c"), spec("v_c"), spec("g_c")],
        out_specs=pl.BlockSpec((1, L, LANES), lambda b, p: (b, 0, p)),
        out_shape=jax.ShapeDtypeStruct((B, L, d_c), BF16),
        compiler_params=_cparams(("parallel", "parallel")),
        name="softmax_attn_ctx",
    )(z, z, z, z)


def _na_kernel(q_ref, k_ref, v_ref, g_ref, kc_ref, vc_ref, bias_ref, o_ref, *, rows):
    lane = lax.broadcasted_iota(jnp.int32, (1, LANES), 1)
    n_loc = WIN_R * GRID_W
    P = kc_ref.shape[2]
    kc = kc_ref[0, 0]
    vc_aug = jnp.concatenate([vc_ref[0, 0], jnp.ones((P, LANES), BF16)], axis=1)
    ones_loc = jnp.ones((n_loc, LANES), BF16)
    tq = NA_ROWS_PER_BLOCK * GRID_W
    lo = lane < HEAD_DIM

    def block(i, carry):
        q0 = pl.multiple_of(i * tq, tq)
        q = q_ref[0, pl.ds(q0, tq), :]
        zero = jnp.zeros_like(q)
        q_heads = (jnp.where(lo, q, zero), jnp.where(lo, zero, q))
        q2 = jnp.concatenate([qh[j * GRID_W:(j + 1) * GRID_W] for j in range(NA_ROWS_PER_BLOCK) for qh in q_heads],
                             axis=0)
        s_ctx = _dot_nt(q2, kc)
        m_ctx = jnp.max(s_ctx, axis=-1, keepdims=True)
        o_rows, p_ctx_rows = [], []
        for j in range(NA_ROWS_PER_BLOCK):
            r = i * NA_ROWS_PER_BLOCK + j
            rs = jnp.clip(r - WIN_R // 2, 0, rows - WIN_R)
            off = rs - r + WIN_R - 1
            k0 = pl.multiple_of(rs * GRID_W, GRID_W)
            kl = k_ref[0, pl.ds(k0, n_loc), :]
            vl_aug = jnp.concatenate([v_ref[0, pl.ds(k0, n_loc), :], ones_loc], axis=1)
            rsl = slice(2 * j * GRID_W, 2 * (j + 1) * GRID_W)
            s_loc = _dot_nt(q2[rsl], kl) + bias_ref[0, :, off].reshape(2 * GRID_W, n_loc)
            m = jnp.maximum(jnp.max(s_loc, axis=-1, keepdims=True), m_ctx[rsl])
            p_ctx_rows.append(jnp.exp2(s_ctx[rsl] - m).astype(BF16))
            o_rows.append(_dot(jnp.exp2(s_loc - m).astype(BF16), vl_aug))
        o = jnp.concatenate(o_rows, axis=0) + _dot(jnp.concatenate(p_ctx_rows, axis=0), vc_aug)
        o = o[:, :LANES] * (1.0 / o[:, LANES:LANES + 1])
        o = jnp.concatenate(
            [jnp.where(lo, o[2 * j * GRID_W:(2 * j + 1) * GRID_W], o[(2 * j + 1) * GRID_W:(2 * j + 2) * GRID_W])
             for j in range(NA_ROWS_PER_BLOCK)], axis=0)
        y = o * _silu(g_ref[0, pl.ds(q0, tq), :].astype(F32))
        o_ref[0, pl.ds(q0, tq), :] = y.astype(o_ref.dtype)
        return carry

    n_blocks = rows // NA_ROWS_PER_BLOCK
    lax.fori_loop(0, n_blocks, block, 0, unroll=math.gcd(n_blocks, NA_BLOCK_UNROLL))


def _na_attn(z, blk, d_c, nk, nv, bias_tab, layer):
    B, L, _ = z.shape
    rows = L // GRID_W
    assert rows >= WIN_R and L % GRID_W == 0 and rows % NA_ROWS_PER_BLOCK == 0
    P = nk.shape[2]
    n_pairs = d_c // LANES
    spec = lambda name: pl.BlockSpec((1, L, LANES), lambda b, p, c=blk[name] * n_pairs: (b, 0, c + p))
    cspec = pl.BlockSpec((1, 1, P, LANES), lambda b, p: (b, layer, 0, p))
    return pl.pallas_call(
        functools.partial(_na_kernel, rows=rows),
        grid=(B, n_pairs),
        in_specs=[
            spec("q_c"), spec("k_c"), spec("v_c"), spec("g_c"), cspec, cspec,
            pl.BlockSpec((1, 2, WIN_R, GRID_W, WIN_R * GRID_W), lambda b, p: (layer, p, 0, 0, 0)),
        ],
        out_specs=pl.BlockSpec((1, L, LANES), lambda b, p: (b, 0, p)),
        out_shape=jax.ShapeDtypeStruct((B, L, d_c), BF16),
        compiler_params=_cparams(("parallel", "parallel")),
        name="na_attn_latent",
    )(z, z, z, z, nk, nv, bias_tab)


def _spatial_gate(u_ref, v_ref, g_ref, ng_ref, nb_ref, ws_ref, bs_ref):
    tm, d_a = v_ref.shape[1], v_ref.shape[2]
    n_groups = d_a // SG_GROUP_W
    vn = (_layernorm(v_ref[0].astype(F32)) * ng_ref[...] + nb_ref[...]).astype(BF16)
    ys = []
    for n in range(tm // CHUNK):
        rows = slice(n * CHUNK, (n + 1) * CHUNK)
        sv = jnp.concatenate(
            [_dot(ws_ref[g], vn[rows, g * SG_GROUP_W:(g + 1) * SG_GROUP_W]) for g in range(n_groups)],
            axis=1) + bs_ref[...]
        ys.append((u_ref[0, rows, :].astype(F32) * sv * _silu(g_ref[0, rows, :].astype(F32))).astype(BF16))
    return jnp.concatenate(ys, axis=0)


def _merge_kernel(x_ref, h_ref, u_ref, v_ref, g_ref, yb_ref, yc_ref, mod_ref, ng_ref, nb_ref, ws_ref, bs_ref,
                  wmg_ref, bmg_ref, wa_ref, wb_ref, wc_ref, wo_ref, lg_ref, lb_ref, o_ref, *, alpha):
    D = x_ref.shape[2]
    h = h_ref[0]
    ya = _spatial_gate(u_ref, v_ref, g_ref, ng_ref, nb_ref, ws_ref, bs_ref)
    m = None
    for i, (y, w_ref) in enumerate(((ya, wa_ref), (yb_ref[0], wb_ref), (yc_ref[0], wc_ref))):
        gate = _sigmoid(_dot(h, wmg_ref[:, i * D:(i + 1) * D]) + bmg_ref[:, i * D:(i + 1) * D])
        t = gate * _dot(y, w_ref[...])
        m = t if m is None else m + t
    out = _dot(m.astype(BF16), wo_ref[...])
    t = alpha * x_ref[0] + mod_ref[0, 0, 2:3, :] * out
    o_ref[0] = _layernorm(t) * lg_ref[...] + lb_ref[...]


def _merge(x, h, z, blk, yb, yc, mod, mod_row0, sg, wmg, bmg, wa, wb, wc, wo, lg, lb, alpha):
    B, L, D = x.shape
    d_a = wa.shape[0]
    tm = min(L, 512)
    assert tm % CHUNK == 0
    tok = lambda w: pl.BlockSpec((1, tm, w), lambda b, i: (b, i, 0))
    zspec = lambda name: pl.BlockSpec((1, tm, d_a), lambda b, i, c=blk[name]: (b, i, c))
    const = lambda a: pl.BlockSpec(a.shape, lambda b, i: (0,) * a.ndim, pipeline_mode=pl.Buffered(1))
    weights = (*sg, wmg, bmg, wa, wb, wc, wo, lg, lb)
    return pl.pallas_call(
        functools.partial(_merge_kernel, alpha=alpha),
        grid=(B, L // tm),
        in_specs=[
            tok(D), tok(D), zspec("u_a"), zspec("v_a"), zspec("g_a"), tok(yb.shape[2]), tok(yc.shape[2]),
            pl.BlockSpec((1, 1, 3, D), lambda b, i: (mod_row0 + b, 0, 0, 0)),
            *[const(w) for w in weights],
        ],
        out_specs=tok(D),
        out_shape=jax.ShapeDtypeStruct((B, L, D), F32),
        compiler_params=_cparams(("parallel", "parallel")),
        name="merge_postnorm",
    )(x, h, z, z, z, yb, yc, mod, *weights)


def _rope_tables(n_tok):
    half = HEAD_DIM // 2
    quarter = HEAD_DIM // 4
    t = jnp.arange(n_tok)
    inv = 1.0 / (ROPE_THETA ** (jnp.arange(0, half, 2, dtype=F32) / half))
    ang = jnp.stack([(t // GRID_W).astype(F32)[:, None] * inv, (t % GRID_W).astype(F32)[:, None] * inv], axis=1)
    cos = jnp.cos(ang)[:, :, None, :]
    sin = jnp.sin(ang)[:, :, None, :]
    zero = jnp.zeros_like(sin)
    shape = (n_tok, 2, 2, quarter)
    c = jnp.broadcast_to(cos, shape).reshape(n_tok, HEAD_DIM)
    s_up = jnp.concatenate([-sin, zero], axis=2).reshape(n_tok, HEAD_DIM)
    s_dn = jnp.concatenate([zero, sin], axis=2).reshape(n_tok, HEAD_DIM)
    rep = LANES // HEAD_DIM
    return tuple(jnp.tile(a, (1, rep)) for a in (c, s_up, s_dn))


def _na_bias_kernel(rb_ref, o_ref):
    shape = (GRID_W, LANES)
    qcol = lax.broadcasted_iota(jnp.int32, shape, 0)
    lane = lax.broadcasted_iota(jnp.int32, shape, 1)
    kcol = lane & (GRID_W - 1)
    cstart = jnp.clip(qcol - WIN_W // 2, 0, GRID_W - WIN_W)
    in_window = (kcol >= cstart) & (kcol < cstart + WIN_W)

    def toeplitz(y, lane0):
        row = jnp.broadcast_to(rb_ref[0, y:y + 1, :], shape)
        return pltpu.roll(row, (lane0 - (WIN_W - 1)) % LANES, 1, stride=1, stride_axis=0)

    n_dy = 2 * WIN_R - 1
    t_lo = [toeplitz(y, 0) for y in range(n_dy)]
    t_hi = [toeplitz(y, GRID_W) for y in range(n_dy)]
    for off in range(WIN_R):
        for jp in range(WIN_R // 2):
            y = off + 2 * jp
            tile = jnp.where(lane < GRID_W, t_lo[y], t_hi[y + 1]) * LOG2E
            o_ref[0, 0, off, :, jp * LANES:(jp + 1) * LANES] = jnp.where(in_window, tile, NEG_INF)


def _na_bias_tables(rel_bias):
    depth, n_heads, n_dy, n_dx = rel_bias.shape
    assert n_dy == 2 * WIN_R - 1 and n_dx == 2 * WIN_W - 1 and WIN_R % 2 == 0
    rb = jnp.pad(rel_bias.astype(F32), ((0, 0), (0, 0), (0, 16 - n_dy), (0, LANES - n_dx)))
    return pl.pallas_call(
        _na_bias_kernel,
        grid=(depth, n_heads),
        in_specs=[pl.BlockSpec((1, 16, LANES), lambda l, h: (l * n_heads + h, 0, 0))],
        out_specs=pl.BlockSpec((1, 1, WIN_R, GRID_W, WIN_R * GRID_W), lambda l, h: (l, h, 0, 0, 0)),
        out_shape=jax.ShapeDtypeStruct((depth, n_heads, WIN_R, GRID_W, WIN_R * GRID_W), F32),
        compiler_params=_cparams(("parallel", "parallel")),
        name="na_bias_table",
    )(rb.reshape(depth * n_heads, 16, LANES))


def kernel(x_prompt, x_sample, c, cache_diff_k, cache_diff_v, cache_na_k, cache_na_v, c_ctx, w_ada, b_ada, w_in, sg_norm_g, sg_norm_b, w_spatial, b_spatial, lambda_q1, lambda_k1, lambda_q2, lambda_k2, diff_subln_g, na_rel_bias, w_br_a, w_br_b, w_br_c, w_mgate, b_mgate, w_out, ln_g, ln_b):
    depth, D, d_in = w_in.shape
    batch, seq, _ = x_prompt.shape
    dec_batch, dec_seq, _ = x_sample.shape
    past = cache_diff_k.shape[4]
    d_a, d_b, d_c = w_br_a.shape[1], w_br_b.shape[1], w_br_c.shape[1]
    h_b = d_b // (2 * HEAD_DIM)
    h_c = d_c // HEAD_DIM
    tn = d_a
    assert d_a == d_b == d_c and d_in == 11 * tn and tn % (2 * LANES) == 0
    assert d_a // SG_GROUP_W == w_spatial.shape[1] and w_spatial.shape[2] == CHUNK
    names = ("u_a", "v_a", "g_a", "q_b", "k_b", "v_b", "g_b", "q_c", "k_c", "v_c", "g_c")
    blk = {n: i for i, n in enumerate(names)}
    alpha = (2 * depth) ** 0.25
    lam_inits = tuple(0.8 - 0.6 * math.exp(-0.3 * l) for l in range(depth))

    mod_rows = -(-(1 + dec_batch) // 8) * 8
    cvec = jnp.zeros((mod_rows, D), F32).at[0].set(c_ctx).at[1:1 + dec_batch].set(c)
    mod = _ada_mod(cvec, w_ada, b_ada).reshape(depth * mod_rows, 1, 3, D)
    lam_vec = _diff_lambdas(lambda_q1, lambda_k1, lambda_q2, lambda_k2, lam_inits)

    w_in_b = w_in.astype(BF16)
    w_s_b = w_spatial.astype(BF16)
    b_s_full = jnp.repeat(jnp.swapaxes(b_spatial, 1, 2), SG_GROUP_W, axis=2)
    wmg_b, wa_b, wb_b, wc_b, wo_b = (w.astype(BF16) for w in (w_mgate, w_br_a, w_br_b, w_br_c, w_out))
    ck = cache_diff_k.transpose(0, 1, 4, 2, 3, 5).reshape(dec_batch, depth, past, d_b).astype(BF16)
    cv = cache_diff_v.transpose(0, 1, 3, 2, 4).reshape(dec_batch, depth, past, d_b).astype(BF16)
    nk = cache_na_k.transpose(0, 1, 3, 2, 4).reshape(dec_batch, depth, past, d_c).astype(BF16)
    nv = cache_na_v.transpose(0, 1, 3, 2, 4).reshape(dec_batch, depth, past, d_c).astype(BF16)
    rope_tabs = _rope_tables(dec_seq)
    bias_tab = _na_bias_tables(na_rel_bias)

    xp = x_prompt.reshape(1, batch * seq, D)
    xs = x_sample
    new_dk, new_dv, new_nk, new_nv = [], [], [], []
    for l in range(depth):
        row2 = lambda a: a[l].reshape(1, -1)
        merge_w = (wmg_b[l], row2(b_mgate), wa_b[l], wb_b[l], wc_b[l], wo_b[l], row2(ln_g), row2(ln_b))
        sg = (row2(sg_norm_g), row2(sg_norm_b), w_s_b[l], b_s_full[l])

        h, z = _inproj(xp, mod, l * mod_rows, w_in_b[l], blk, tn, F32)
        zc = z.reshape(batch, seq, d_in)
        yb = _diff_attn(zc, blk, d_b, lam_vec, row2(diff_subln_g), l, lam_inits[l])
        yc = _softmax_attn(zc, blk, d_c)
        flat = lambda a: a.reshape(1, batch * seq, a.shape[-1])
        xp = _merge(xp, h, z, blk, flat(yb), flat(yc), mod, l * mod_rows, sg, *merge_w, alpha)
        col = lambda n: zc[:, :, blk[n] * tn:(blk[n] + 1) * tn]
        new_dk.append(col("k_b").reshape(batch, seq, 2, h_b, HEAD_DIM).transpose(0, 2, 3, 1, 4))
        new_dv.append(col("v_b").reshape(batch, seq, h_b, 2 * HEAD_DIM).transpose(0, 2, 1, 3))
        new_nk.append(col("k_c").reshape(batch, seq, h_c, HEAD_DIM).transpose(0, 2, 1, 3))
        new_nv.append(col("v_c").reshape(batch, seq, h_c, HEAD_DIM).transpose(0, 2, 1, 3))

        h, z = _inproj(xs, mod, l * mod_rows + 1, w_in_b[l], blk, tn, BF16, rope_tabs)
        yb = _diff_attn(z, blk, d_b, lam_vec, row2(diff_subln_g), l, lam_inits[l], cache=(ck, cv))
        yc = _na_attn(z, blk, d_c, nk, nv, bias_tab, l)
        xs = _merge(xs, h, z, blk, yb, yc, mod, l * mod_rows + 1, sg, *merge_w, alpha)

    return (xp.reshape(batch, seq, D), xs,
            jnp.stack(new_dk, axis=1), jnp.stack(new_dv, axis=1),
            jnp.stack(new_nk, axis=1), jnp.stack(new_nv, axis=1))
```

```python
import functools
import math

import jax
import jax.numpy as jnp
from jax import lax
from jax.experimental import pallas as pl
from jax.experimental.pallas import tpu as pltpu

GRID_W = 64
CHUNK = 128
HEAD_DIM = 64
SG_GROUP_W = 128
WIN_R = 8
WIN_W = 16
DA_TQ = 256
DA_TK = 512
DA_SUBTILES = 2
INPROJ_SUBTILES = 2
MERGE_TM = 512
MERGE_SUBTILES = 2
NA_ROWS_PER_BLOCK = 4
NA_BLOCK_UNROLL = 8
ROPE_THETA = 10000.0
EPS = 1e-6
NEG_INF = -1e30
LANES = 128
LOG2E = math.log2(math.e)
Q_SCALE = HEAD_DIM ** -0.5 * LOG2E

BF16 = jnp.bfloat16
F32 = jnp.float32

VMEM_LIMIT = 56 * 1024 * 1024


def _cparams(sem):
    return pltpu.CompilerParams(dimension_semantics=sem, vmem_limit_bytes=VMEM_LIMIT)


def _sigmoid(x):
    return 1.0 / (1.0 + jnp.exp(-x))


def _silu(x):
    return x * _sigmoid(x)


def _dot(a, b):
    return jnp.dot(a, b, preferred_element_type=F32)


def _dot_nt(a, b):
    return lax.dot_general(a, b, (((1,), (1,)), ((), ())), preferred_element_type=F32)


def _layernorm(x):
    mu = jnp.mean(x, axis=-1, keepdims=True)
    xc = x - mu
    var = jnp.mean(xc * xc, axis=-1, keepdims=True)
    return xc * lax.rsqrt(var + EPS)


def _ada_kernel(c_ref, w_ref, b_ref, o_ref):
    s = _silu(c_ref[...]).astype(BF16)
    o_ref[0] = _dot(s, w_ref[0].astype(BF16)) + b_ref[0]


def _ada_mod(cvec, w_ada, b_ada):
    depth, d, d3 = w_ada.shape
    rows = cvec.shape[0]
    tn = 1024
    return pl.pallas_call(
        _ada_kernel,
        grid=(depth, d3 // tn),
        in_specs=[
            pl.BlockSpec((rows, d), lambda l, j: (0, 0)),
            pl.BlockSpec((1, d, tn), lambda l, j: (l, 0, j)),
            pl.BlockSpec((1, 1, tn), lambda l, j: (l, 0, j)),
        ],
        out_specs=pl.BlockSpec((1, rows, tn), lambda l, j: (l, 0, j)),
        out_shape=jax.ShapeDtypeStruct((depth, rows, d3), F32),
        compiler_params=_cparams(("parallel", "parallel")),
        name="ada_mod",
    )(cvec, w_ada, b_ada.reshape(depth, 1, d3))


def _lambda_kernel(lam_ref, o_ref, *, lam_inits):
    p = lam_ref[...]
    t1 = jnp.sum(p[:, 0, :] * p[:, 1, :], axis=-1, keepdims=True)
    t2 = jnp.sum(p[:, 2, :] * p[:, 3, :], axis=-1, keepdims=True)
    layer = lax.broadcasted_iota(jnp.int32, t1.shape, 0)
    init = jnp.zeros_like(t1)
    for l, v in enumerate(lam_inits):
        init = jnp.where(layer == l, v, init)
    o_ref[...] = jnp.broadcast_to(jnp.exp(t1) - jnp.exp(t2) + init, o_ref.shape)


def _diff_lambdas(lq1, lk1, lq2, lk2, lam_inits):
    depth = lq1.shape[0]
    params = jnp.stack([lq1, lk1, lq2, lk2], axis=1).astype(F32)
    out = pl.pallas_call(
        functools.partial(_lambda_kernel, lam_inits=lam_inits),
        out_shape=jax.ShapeDtypeStruct((depth, LANES), F32),
        name="diff_lambda",
    )(params)
    return out[:, 0]


def _rope(a, c, s_up, s_dn):
    outs = []
    for i in range(a.shape[1] // LANES):
        g = a[:, i * LANES:(i + 1) * LANES]
        outs.append(g * c + pltpu.roll(g, LANES - 16, 1) * s_up + pltpu.roll(g, 16, 1) * s_dn)
    return jnp.concatenate(outs, axis=1)


def _inproj_kernel(*refs, rope, blk, tn, ts):
    if rope:
        x_ref, mod_ref, w_ref, c_ref, su_ref, sd_ref, h_ref, z_ref = refs
    else:
        x_ref, mod_ref, w_ref, h_ref, z_ref = refs
    for sub in range(x_ref.shape[1] // ts):
        rows = slice(sub * ts, (sub + 1) * ts)
        xn = _layernorm(x_ref[0, rows, :])
        hb = (xn * (1.0 + mod_ref[0, 0, 1:2, :]) + mod_ref[0, 0, 0:1, :]).astype(BF16)
        h_ref[0, rows, :] = hb
        for j in range(w_ref.shape[1] // tn):
            cols = slice(j * tn, (j + 1) * tn)
            acc = _dot(hb, w_ref[:, cols])
            if rope and j in (blk["q_b"], blk["k_b"]):
                acc = _rope(acc, c_ref[rows, :], su_ref[rows, :], sd_ref[rows, :])
            if j in (blk["q_b"], blk["q_c"]):
                acc = acc * Q_SCALE
            z_ref[0, rows, cols] = acc.astype(z_ref.dtype)


def _inproj(x, mod, mod_row0, w, blk, tn, z_dtype, rope_tabs=None):
    B, L, D = x.shape
    d_in = w.shape[1]
    tm = min(L, 512 if z_dtype == BF16 else 256)
    rope = rope_tabs is not None
    in_specs = [
        pl.BlockSpec((1, tm, D), lambda b, i: (b, i, 0)),
        pl.BlockSpec((1, 1, 3, D), lambda b, i: (mod_row0 + b, 0, 0, 0)),
        pl.BlockSpec((D, d_in), lambda b, i: (0, 0), pipeline_mode=pl.Buffered(1)),
    ]
    args = [x, mod, w]
    if rope:
        in_specs += [pl.BlockSpec((tm, LANES), lambda b, i: (i, 0))] * 3
        args += list(rope_tabs)
    return pl.pallas_call(
        functools.partial(_inproj_kernel, rope=rope, blk=blk, tn=tn, ts=tm // INPROJ_SUBTILES),
        grid=(B, L // tm),
        in_specs=in_specs,
        out_specs=[
            pl.BlockSpec((1, tm, D), lambda b, i: (b, i, 0)),
            pl.BlockSpec((1, tm, d_in), lambda b, i: (b, i, 0)),
        ],
        out_shape=[
            jax.ShapeDtypeStruct((B, L, D), BF16),
            jax.ShapeDtypeStruct((B, L, d_in), z_dtype),
        ],
        compiler_params=_cparams(("parallel", "parallel")),
        name="inproj_rope" if rope else "inproj",
    )(*args)


def _head_lanes(hh):
    lane = lax.broadcasted_iota(jnp.int32, (1, LANES), 1)
    return (lane < HEAD_DIM) if hh == 0 else (lane >= HEAD_DIM)


def _diff_head(q1, q2, chunks, hh, lam, lam_init, sg, g):
    tq = q1.shape[0]
    dv = 2 * HEAD_DIM
    qm = [jnp.where(_head_lanes(hh), q, jnp.zeros_like(q)).astype(BF16) for q in (q1, q2)]
    state = [(jnp.full((tq, 1), -jnp.inf, F32), jnp.zeros((tq, 2 * dv), F32)) for _ in range(2)]
    for k1c, k2c, vc in chunks:
        ones = jnp.ones((vc.shape[0], LANES), BF16)
        vaug = jnp.concatenate([vc[:, hh * dv:(hh + 1) * dv].astype(BF16), ones], axis=1)
        for m, kc in enumerate((k1c, k2c)):
            mx, acc = state[m]
            s = _dot_nt(qm[m], kc[...].astype(BF16))
            mx_new = jnp.maximum(mx, jnp.max(s, axis=-1, keepdims=True))
            p = jnp.exp2(s - mx_new).astype(BF16)
            state[m] = (mx_new, jnp.exp2(mx - mx_new) * acc + _dot(p, vaug))
    (_, a1), (_, a2) = state
    o = a1[:, :dv] * (1.0 / a1[:, dv:dv + 1]) - a2[:, :dv] * (lam / a2[:, dv:dv + 1])
    of = o * lax.rsqrt(jnp.mean(o * o, axis=-1, keepdims=True) + EPS)
    return of * sg * (1.0 - lam_init) * _silu(g.astype(F32))


def _diff_attn_kernel(lam_ref, q1_ref, q2_ref, k1_ref, k2_ref, v_ref, g_ref, kc1_ref, kc2_ref, vc_ref, sg_ref, o_ref,
                      *, layer, lam_init, tq, tk, n_self, n_ctx):
    dv = 2 * HEAD_DIM
    chunks = [(k1_ref.at[0, c * tk:(c + 1) * tk, :], k2_ref.at[0, c * tk:(c + 1) * tk, :],
               v_ref.at[0, c * tk:(c + 1) * tk, :]) for c in range(n_self)]
    chunks += [(kc1_ref.at[0, 0, c * tk:(c + 1) * tk, :], kc2_ref.at[0, 0, c * tk:(c + 1) * tk, :],
                vc_ref.at[0, 0, c * tk:(c + 1) * tk, :]) for c in range(n_ctx)]
    for sub in range(q1_ref.shape[1] // tq):
        rows = slice(sub * tq, (sub + 1) * tq)
        for hh in range(2):
            cols = slice(hh * dv, (hh + 1) * dv)
            y = _diff_head(q1_ref[0, rows, :], q2_ref[0, rows, :], chunks, hh, lam_ref[layer], lam_init,
                           sg_ref[...], g_ref[0, rows, cols])
            o_ref[0, rows, cols] = y.astype(o_ref.dtype)


def _diff_attn(z, blk, d_b, lam_vec, subln_g, layer, lam_init, cache):
    B, L, _ = z.shape
    tq = min(L, DA_TQ)
    bq = min(L, DA_SUBTILES * tq)
    tk = min(L, DA_TK)
    n_pairs = d_b // (4 * HEAD_DIM)
    col128 = lambda name: blk[name] * (d_b // LANES)
    col256 = lambda name: blk[name] * (d_b // (2 * LANES))
    qspec = lambda m: pl.BlockSpec((1, bq, LANES), lambda b, p, i: (b, i, col128("q_b") + m * n_pairs + p))
    kspec = lambda m: pl.BlockSpec((1, L, LANES), lambda b, p, i: (b, 0, col128("k_b") + m * n_pairs + p))
    in_specs = [
        pl.BlockSpec(memory_space=pltpu.SMEM),
        qspec(0), qspec(1), kspec(0), kspec(1),
        pl.BlockSpec((1, L, 2 * LANES), lambda b, p, i: (b, 0, col256("v_b") + p)),
        pl.BlockSpec((1, bq, 2 * LANES), lambda b, p, i: (b, i, col256("g_b") + p)),
    ]
    ck, cv = cache
    P = ck.shape[2]
    assert P % tk == 0 and L % tk == 0 and L % bq == 0
    in_specs += [
        pl.BlockSpec((1, 1, P, LANES), lambda b, p, i: (b, layer, 0, p)),
        pl.BlockSpec((1, 1, P, LANES), lambda b, p, i: (b, layer, 0, n_pairs + p)),
        pl.BlockSpec((1, 1, P, 2 * LANES), lambda b, p, i: (b, layer, 0, p)),
        pl.BlockSpec((1, 2 * HEAD_DIM), lambda b, p, i: (0, 0)),
    ]
    return pl.pallas_call(
        functools.partial(_diff_attn_kernel, layer=layer, lam_init=lam_init, tq=tq, tk=tk, n_self=L // tk,
                          n_ctx=P // tk),
        grid=(B, n_pairs, L // bq),
        in_specs=in_specs,
        out_specs=pl.BlockSpec((1, bq, 2 * LANES), lambda b, p, i: (b, i, p)),
        out_shape=jax.ShapeDtypeStruct((B, L, d_b), BF16),
        compiler_params=_cparams(("parallel", "parallel", "parallel")),
        name="diff_attn_latent",
    )(lam_vec, z, z, z, z, z, z, ck, ck, cv, subln_g)


def _ctx_attn_kernel(lam_ref, qb_ref, kb_ref, vb_ref, gb_ref, qc_ref, kc_ref, vc_ref, gc_ref, sg_ref,
                     yb_ref, yc_ref, *, layer, lam_init):
    dv = 2 * HEAD_DIM
    n_pairs_b = qb_ref.shape[2] // (2 * LANES)
    for p in range(n_pairs_b):
        blk1 = slice(p * LANES, (p + 1) * LANES)
        blk2 = slice((n_pairs_b + p) * LANES, (n_pairs_b + p + 1) * LANES)
        vcols = slice(p * 2 * dv, (p + 1) * 2 * dv)
        chunks = [(kb_ref.at[0, :, blk1], kb_ref.at[0, :, blk2], vb_ref.at[0, :, vcols])]
        for hh in range(2):
            cols = slice((2 * p + hh) * dv, (2 * p + hh + 1) * dv)
            y = _diff_head(qb_ref[0, :, blk1], qb_ref[0, :, blk2], chunks, hh, lam_ref[layer], lam_init,
                           sg_ref[...], gb_ref[0, :, cols])
            yb_ref[0, :, cols] = y.astype(yb_ref.dtype)

    lo = _head_lanes(0)
    for p in range(qc_ref.shape[2] // LANES):
        cols = slice(p * LANES, (p + 1) * LANES)
        q = qc_ref[0, :, cols]
        k = kc_ref[0, :, cols].astype(BF16)
        vaug = jnp.concatenate([vc_ref[0, :, cols].astype(BF16), jnp.ones((k.shape[0], LANES), BF16)], axis=1)
        outs = []
        for hh in range(2):
            qm = jnp.where(_head_lanes(hh), q, jnp.zeros_like(q)).astype(BF16)
            s = _dot_nt(qm, k)
            o = _dot(jnp.exp2(s - jnp.max(s, axis=-1, keepdims=True)).astype(BF16), vaug)
            outs.append(o[:, :LANES] * (1.0 / o[:, LANES:LANES + 1]))
        o = jnp.where(lo, outs[0], outs[1])
        yc_ref[0, :, cols] = (o * _silu(gc_ref[0, :, cols].astype(F32))).astype(yc_ref.dtype)


def _ctx_attn(z, blk, d_b, d_c, lam_vec, subln_g, layer, lam_init):
    B, L, _ = z.shape
    assert d_b == d_c
    spec = lambda name: pl.BlockSpec((1, L, d_b), lambda b, c=blk[name]: (b, 0, c))
    names = ("q_b", "k_b", "v_b", "g_b", "q_c", "k_c", "v_c", "g_c")
    return pl.pallas_call(
        functools.partial(_ctx_attn_kernel, layer=layer, lam_init=lam_init),
        grid=(B,),
        in_specs=[pl.BlockSpec(memory_space=pltpu.SMEM), *[spec(n) for n in names],
                  pl.BlockSpec((1, 2 * HEAD_DIM), lambda b: (0, 0))],
        out_specs=[pl.BlockSpec((1, L, d_b), lambda b: (b, 0, 0)), pl.BlockSpec((1, L, d_c), lambda b: (b, 0, 0))],
        out_shape=[jax.ShapeDtypeStruct((B, L, d_b), BF16), jax.ShapeDtypeStruct((B, L, d_c), BF16)],
        compiler_params=_cparams(("parallel",)),
        name="ctx_attn",
    )(lam_vec, *([z] * len(names)), subln_g)


def _na_kernel(q_ref, k_ref, v_ref, g_ref, kc_ref, vc_ref, bias_ref, o_ref, *, rows):
    lane = lax.broadcasted_iota(jnp.int32, (1, LANES), 1)
    n_loc = WIN_R * GRID_W
    P = kc_ref.shape[2]
    kc = kc_ref[0, 0]
    vc_aug = jnp.concatenate([vc_ref[0, 0], jnp.ones((P, LANES), BF16)], axis=1)
    ones_loc = jnp.ones((n_loc, LANES), BF16)
    tq = NA_ROWS_PER_BLOCK * GRID_W
    lo = lane < HEAD_DIM

    def block(i, carry):
        q0 = pl.multiple_of(i * tq, tq)
        q = q_ref[0, pl.ds(q0, tq), :]
        zero = jnp.zeros_like(q)
        q_heads = (jnp.where(lo, q, zero), jnp.where(lo, zero, q))
        q2 = jnp.concatenate([qh[j * GRID_W:(j + 1) * GRID_W] for j in range(NA_ROWS_PER_BLOCK) for qh in q_heads],
                             axis=0)
        s_ctx = _dot_nt(q2, kc)
        m_ctx = jnp.max(s_ctx, axis=-1, keepdims=True)
        o_rows, p_ctx_rows = [], []
        for j in range(NA_ROWS_PER_BLOCK):
            r = i * NA_ROWS_PER_BLOCK + j
            rs = jnp.clip(r - WIN_R // 2, 0, rows - WIN_R)
            off = rs - r + WIN_R - 1
            k0 = pl.multiple_of(rs * GRID_W, GRID_W)
            kl = k_ref[0, pl.ds(k0, n_loc), :]
            vl_aug = jnp.concatenate([v_ref[0, pl.ds(k0, n_loc), :], ones_loc], axis=1)
            rsl = slice(2 * j * GRID_W, 2 * (j + 1) * GRID_W)
            s_loc = _dot_nt(q2[rsl], kl) + bias_ref[0, :, off].reshape(2 * GRID_W, n_loc)
            m = jnp.maximum(jnp.max(s_loc, axis=-1, keepdims=True), m_ctx[rsl])
            p_ctx_rows.append(jnp.exp2(s_ctx[rsl] - m).astype(BF16))
            o_rows.append(_dot(jnp.exp2(s_loc - m).astype(BF16), vl_aug))
        o = jnp.concatenate(o_rows, axis=0) + _dot(jnp.concatenate(p_ctx_rows, axis=0), vc_aug)
        o = o[:, :LANES] * (1.0 / o[:, LANES:LANES + 1])
        o = jnp.concatenate(
            [jnp.where(lo, o[2 * j * GRID_W:(2 * j + 1) * GRID_W], o[(2 * j + 1) * GRID_W:(2 * j + 2) * GRID_W])
             for j in range(NA_ROWS_PER_BLOCK)], axis=0)
        y = o * _silu(g_ref[0, pl.ds(q0, tq), :].astype(F32))
        o_ref[0, pl.ds(q0, tq), :] = y.astype(o_ref.dtype)
        return carry

    n_blocks = rows // NA_ROWS_PER_BLOCK
    lax.fori_loop(0, n_blocks, block, 0, unroll=math.gcd(n_blocks, NA_BLOCK_UNROLL))


def _na_attn(z, blk, d_c, nk, nv, bias_tab, layer):
    B, L, _ = z.shape
    rows = L // GRID_W
    assert rows >= WIN_R and L % GRID_W == 0 and rows % NA_ROWS_PER_BLOCK == 0
    P = nk.shape[2]
    n_pairs = d_c // LANES
    spec = lambda name: pl.BlockSpec((1, L, LANES), lambda b, p, c=blk[name] * n_pairs: (b, 0, c + p))
    cspec = pl.BlockSpec((1, 1, P, LANES), lambda b, p: (b, layer, 0, p))
    return pl.pallas_call(
        functools.partial(_na_kernel, rows=rows),
        grid=(B, n_pairs),
        in_specs=[
            spec("q_c"), spec("k_c"), spec("v_c"), spec("g_c"), cspec, cspec,
            pl.BlockSpec((1, 2, WIN_R, GRID_W, WIN_R * GRID_W), lambda b, p: (layer, p, 0, 0, 0)),
        ],
        out_specs=pl.BlockSpec((1, L, LANES), lambda b, p: (b, 0, p)),
        out_shape=jax.ShapeDtypeStruct((B, L, d_c), BF16),
        compiler_params=_cparams(("parallel", "parallel")),
        name="na_attn_latent",
    )(z, z, z, z, nk, nv, bias_tab)


def _spatial_gate(u, v, g, ng_ref, nb_ref, ws_ref, bs_ref):
    n_groups = v.shape[1] // SG_GROUP_W
    vn = (_layernorm(v.astype(F32)) * ng_ref[...] + nb_ref[...]).astype(BF16)
    ys = []
    for n in range(v.shape[0] // CHUNK):
        rows = slice(n * CHUNK, (n + 1) * CHUNK)
        sv = jnp.concatenate(
            [_dot(ws_ref[gi], vn[rows, gi * SG_GROUP_W:(gi + 1) * SG_GROUP_W]) for gi in range(n_groups)],
            axis=1) + bs_ref[...]
        ys.append((u[rows].astype(F32) * sv * _silu(g[rows].astype(F32))).astype(BF16))
    return jnp.concatenate(ys, axis=0)


def _merge_kernel(x_ref, h_ref, u_ref, v_ref, g_ref, yb_ref, yc_ref, mod_ref, ng_ref, nb_ref, ws_ref, bs_ref,
                  wmg_ref, bmg_ref, wa_ref, wb_ref, wc_ref, wo_ref, lg_ref, lb_ref, o_ref, *, alpha, ts):
    D = x_ref.shape[2]
    for sub in range(x_ref.shape[1] // ts):
        rows = slice(sub * ts, (sub + 1) * ts)
        h = h_ref[0, rows, :]
        ya = _spatial_gate(u_ref[0, rows, :], v_ref[0, rows, :], g_ref[0, rows, :], ng_ref, nb_ref, ws_ref, bs_ref)
        m = None
        for i, (y, w_ref) in enumerate(((ya, wa_ref), (yb_ref[0, rows, :], wb_ref), (yc_ref[0, rows, :], wc_ref))):
            gate = _sigmoid(_dot(h, wmg_ref[:, i * D:(i + 1) * D]) + bmg_ref[:, i * D:(i + 1) * D])
            t = gate * _dot(y, w_ref[...])
            m = t if m is None else m + t
        out = _dot(m.astype(BF16), wo_ref[...])
        t = alpha * x_ref[0, rows, :] + mod_ref[0, 0, 2:3, :] * out
        o_ref[0, rows, :] = _layernorm(t) * lg_ref[...] + lb_ref[...]


def _merge(x, h, z, blk, yb, yc, mod, mod_row0, sg, wmg, bmg, wa, wb, wc, wo, lg, lb, alpha):
    B, L, D = x.shape
    d_a = wa.shape[0]
    tm = min(L, MERGE_TM)
    ts = tm // MERGE_SUBTILES
    assert ts % CHUNK == 0 and L % tm == 0
    tok = lambda w: pl.BlockSpec((1, tm, w), lambda b, i: (b, i, 0))
    zspec = lambda name: pl.BlockSpec((1, tm, d_a), lambda b, i, c=blk[name]: (b, i, c))
    const = lambda a: pl.BlockSpec(a.shape, lambda b, i: (0,) * a.ndim, pipeline_mode=pl.Buffered(1))
    weights = (*sg, wmg, bmg, wa, wb, wc, wo, lg, lb)
    return pl.pallas_call(
        functools.partial(_merge_kernel, alpha=alpha, ts=ts),
        grid=(B, L // tm),
        in_specs=[
            tok(D), tok(D), zspec("u_a"), zspec("v_a"), zspec("g_a"), tok(yb.shape[2]), tok(yc.shape[2]),
            pl.BlockSpec((1, 1, 3, D), lambda b, i: (mod_row0 + b, 0, 0, 0)),
            *[const(w) for w in weights],
        ],
        out_specs=tok(D),
        out_shape=jax.ShapeDtypeStruct((B, L, D), F32),
        compiler_params=_cparams(("parallel", "parallel")),
        name="merge_postnorm",
    )(x, h, z, z, z, yb, yc, mod, *weights)


def _rope_tables(n_tok):
    half = HEAD_DIM // 2
    quarter = HEAD_DIM // 4
    t = jnp.arange(n_tok)
    inv = 1.0 / (ROPE_THETA ** (jnp.arange(0, half, 2, dtype=F32) / half))
    ang = jnp.stack([(t // GRID_W).astype(F32)[:, None] * inv, (t % GRID_W).astype(F32)[:, None] * inv], axis=1)
    cos = jnp.cos(ang)[:, :, None, :]
    sin = jnp.sin(ang)[:, :, None, :]
    zero = jnp.zeros_like(sin)
    shape = (n_tok, 2, 2, quarter)
    c = jnp.broadcast_to(cos, shape).reshape(n_tok, HEAD_DIM)
    s_up = jnp.concatenate([-sin, zero], axis=2).reshape(n_tok, HEAD_DIM)
    s_dn = jnp.concatenate([zero, sin], axis=2).reshape(n_tok, HEAD_DIM)
    rep = LANES // HEAD_DIM
    return tuple(jnp.tile(a, (1, rep)) for a in (c, s_up, s_dn))


def _na_bias_kernel(rb_ref, o_ref):
    shape = (GRID_W, LANES)
    qcol = lax.broadcasted_iota(jnp.int32, shape, 0)
    lane = lax.broadcasted_iota(jnp.int32, shape, 1)
    kcol = lane & (GRID_W - 1)
    cstart = jnp.clip(qcol - WIN_W // 2, 0, GRID_W - WIN_W)
    in_window = (kcol >= cstart) & (kcol < cstart + WIN_W)

    def toeplitz(y, lane0):
        row = jnp.broadcast_to(rb_ref[0, y:y + 1, :], shape)
        return pltpu.roll(row, (lane0 - (WIN_W - 1)) % LANES, 1, stride=1, stride_axis=0)

    n_dy = 2 * WIN_R - 1
    t_lo = [toeplitz(y, 0) for y in range(n_dy)]
    t_hi = [toeplitz(y, GRID_W) for y in range(n_dy)]
    for off in range(WIN_R):
        for jp in range(WIN_R // 2):
            y = off + 2 * jp
            tile = jnp.where(lane < GRID_W, t_lo[y], t_hi[y + 1]) * LOG2E
            o_ref[0, 0, off, :, jp * LANES:(jp + 1) * LANES] = jnp.where(in_window, tile, NEG_INF)


def _na_bias_tables(rel_bias):
    depth, n_heads, n_dy, n_dx = rel_bias.shape
    assert n_dy == 2 * WIN_R - 1 and n_dx == 2 * WIN_W - 1 and WIN_R % 2 == 0
    rb = jnp.pad(rel_bias.astype(F32), ((0, 0), (0, 0), (0, 16 - n_dy), (0, LANES - n_dx)))
    return pl.pallas_call(
        _na_bias_kernel,
        grid=(depth, n_heads),
        in_specs=[pl.BlockSpec((1, 16, LANES), lambda l, h: (l * n_heads + h, 0, 0))],
        out_specs=pl.BlockSpec((1, 1, WIN_R, GRID_W, WIN_R * GRID_W), lambda l, h: (l, h, 0, 0, 0)),
        out_shape=jax.ShapeDtypeStruct((depth, n_heads, WIN_R, GRID_W, WIN_R * GRID_W), F32),
        compiler_params=_cparams(("parallel", "parallel")),
        name="na_bias_table",
    )(rb.reshape(depth * n_heads, 16, LANES))


def kernel(x_prompt, x_sample, c, cache_diff_k, cache_diff_v, cache_na_k, cache_na_v, c_ctx, w_ada, b_ada, w_in, sg_norm_g, sg_norm_b, w_spatial, b_spatial, lambda_q1, lambda_k1, lambda_q2, lambda_k2, diff_subln_g, na_rel_bias, w_br_a, w_br_b, w_br_c, w_mgate, b_mgate, w_out, ln_g, ln_b):
    depth, D, d_in = w_in.shape
    batch, seq, _ = x_prompt.shape
    dec_batch, dec_seq, _ = x_sample.shape
    past = cache_diff_k.shape[4]
    d_a, d_b, d_c = w_br_a.shape[1], w_br_b.shape[1], w_br_c.shape[1]
    h_b = d_b // (2 * HEAD_DIM)
    h_c = d_c // HEAD_DIM
    tn = d_a
    assert d_a == d_b == d_c and d_in == 11 * tn and tn % (2 * LANES) == 0
    assert d_a // SG_GROUP_W == w_spatial.shape[1] and w_spatial.shape[2] == CHUNK
    names = ("u_a", "v_a", "g_a", "q_b", "k_b", "v_b", "g_b", "q_c", "k_c", "v_c", "g_c")
    blk = {n: i for i, n in enumerate(names)}
    alpha = (2 * depth) ** 0.25
    lam_inits = tuple(0.8 - 0.6 * math.exp(-0.3 * l) for l in range(depth))

    mod_rows = -(-(1 + dec_batch) // 8) * 8
    cvec = jnp.zeros((mod_rows, D), F32).at[0].set(c_ctx).at[1:1 + dec_batch].set(c)
    mod = _ada_mod(cvec, w_ada, b_ada).reshape(depth * mod_rows, 1, 3, D)
    lam_vec = _diff_lambdas(lambda_q1, lambda_k1, lambda_q2, lambda_k2, lam_inits)

    w_in_b = w_in.astype(BF16)
    w_s_b = w_spatial.astype(BF16)
    b_s_full = jnp.repeat(jnp.swapaxes(b_spatial, 1, 2), SG_GROUP_W, axis=2)
    wmg_b, wa_b, wb_b, wc_b, wo_b = (w.astype(BF16) for w in (w_mgate, w_br_a, w_br_b, w_br_c, w_out))
    ck = cache_diff_k.transpose(0, 1, 4, 2, 3, 5).reshape(dec_batch, depth, past, d_b).astype(BF16)
    cv = cache_diff_v.transpose(0, 1, 3, 2, 4).reshape(dec_batch, depth, past, d_b).astype(BF16)
    nk = cache_na_k.transpose(0, 1, 3, 2, 4).reshape(dec_batch, depth, past, d_c).astype(BF16)
    nv = cache_na_v.transpose(0, 1, 3, 2, 4).reshape(dec_batch, depth, past, d_c).astype(BF16)
    rope_tabs = _rope_tables(dec_seq)
    bias_tab = _na_bias_tables(na_rel_bias)

    xp = x_prompt.reshape(1, batch * seq, D)
    xs = x_sample
    new_dk, new_dv, new_nk, new_nv = [], [], [], []
    for l in range(depth):
        row2 = lambda a: a[l].reshape(1, -1)
        merge_w = (wmg_b[l], row2(b_mgate), wa_b[l], wb_b[l], wc_b[l], wo_b[l], row2(ln_g), row2(ln_b))
        sg = (row2(sg_norm_g), row2(sg_norm_b), w_s_b[l], b_s_full[l])

        h, z = _inproj(xp, mod, l * mod_rows, w_in_b[l], blk, tn, F32)
        zc = z.reshape(batch, seq, d_in)
        yb, yc = _ctx_attn(zc, blk, d_b, d_c, lam_vec, row2(diff_subln_g), l, lam_inits[l])
        flat = lambda a: a.reshape(1, batch * seq, a.shape[-1])
        xp = _merge(xp, h, z, blk, flat(yb), flat(yc), mod, l * mod_rows, sg, *merge_w, alpha)
        col = lambda n: zc[:, :, blk[n] * tn:(blk[n] + 1) * tn]
        new_dk.append(col("k_b").reshape(batch, seq, 2, h_b, HEAD_DIM).transpose(0, 2, 3, 1, 4))
        new_dv.append(col("v_b").reshape(batch, seq, h_b, 2 * HEAD_DIM).transpose(0, 2, 1, 3))
        new_nk.append(col("k_c").reshape(batch, seq, h_c, HEAD_DIM).transpose(0, 2, 1, 3))
        new_nv.append(col("v_c").reshape(batch, seq, h_c, HEAD_DIM).transpose(0, 2, 1, 3))

        h, z = _inproj(xs, mod, l * mod_rows + 1, w_in_b[l], blk, tn, BF16, rope_tabs)
        yb = _diff_attn(z, blk, d_b, lam_vec, row2(diff_subln_g), l, lam_inits[l], cache=(ck, cv))
        yc = _na_attn(z, blk, d_c, nk, nv, bias_tab, l)
        xs = _merge(xs, h, z, blk, yb, yc, mod, l * mod_rows + 1, sg, *merge_w, alpha)

    return (xp.reshape(batch, seq, D), xs,
            jnp.stack(new_dk, axis=1), jnp.stack(new_dv, axis=1),
            jnp.stack(new_nk, axis=1), jnp.stack(new_nv, axis=1))
```

```python
import functools
import math

import jax
import jax.numpy as jnp
from jax import lax
from jax.experimental import pallas as pl
from jax.experimental.pallas import tpu as pltpu

GRID_W = 64
CHUNK = 128
HEAD_DIM = 64
SG_GROUP_W = 128
WIN_R = 8
WIN_W = 16
DA_TQ = 256
DA_TK = 512
DA_SUBTILES = 2
DA_ONES_ROWS = 16
INPROJ_SUBTILES = 2
MERGE_TM = 512
MERGE_SUBTILES = 2
NA_ROWS_PER_BLOCK = 4
NA_BLOCK_UNROLL = 8
ROPE_THETA = 10000.0
EPS = 1e-6
NEG_INF = -1e30
LANES = 128
LOG2E = math.log2(math.e)
Q_SCALE = HEAD_DIM ** -0.5 * LOG2E

BF16 = jnp.bfloat16
F32 = jnp.float32

VMEM_LIMIT = 56 * 1024 * 1024


def _cparams(sem):
    return pltpu.CompilerParams(dimension_semantics=sem, vmem_limit_bytes=VMEM_LIMIT)


def _sigmoid(x):
    return 1.0 / (1.0 + jnp.exp(-x))


def _silu(x):
    return x * _sigmoid(x)


def _dot(a, b):
    return jnp.dot(a, b, preferred_element_type=F32)


def _dot_nt(a, b):
    return lax.dot_general(a, b, (((1,), (1,)), ((), ())), preferred_element_type=F32)


def _layernorm(x):
    mu = jnp.mean(x, axis=-1, keepdims=True)
    xc = x - mu
    var = jnp.mean(xc * xc, axis=-1, keepdims=True)
    return xc * lax.rsqrt(var + EPS)


def _ada_kernel(c_ref, w_ref, b_ref, o_ref):
    s = _silu(c_ref[...]).astype(BF16)
    o_ref[0] = _dot(s, w_ref[0].astype(BF16)) + b_ref[0]


def _ada_mod(cvec, w_ada, b_ada):
    depth, d, d3 = w_ada.shape
    rows = cvec.shape[0]
    tn = 1024
    return pl.pallas_call(
        _ada_kernel,
        grid=(depth, d3 // tn),
        in_specs=[
            pl.BlockSpec((rows, d), lambda l, j: (0, 0)),
            pl.BlockSpec((1, d, tn), lambda l, j: (l, 0, j)),
            pl.BlockSpec((1, 1, tn), lambda l, j: (l, 0, j)),
        ],
        out_specs=pl.BlockSpec((1, rows, tn), lambda l, j: (l, 0, j)),
        out_shape=jax.ShapeDtypeStruct((depth, rows, d3), F32),
        compiler_params=_cparams(("parallel", "parallel")),
        name="ada_mod",
    )(cvec, w_ada, b_ada.reshape(depth, 1, d3))


def _lambda_kernel(lam_ref, o_ref, *, lam_inits):
    p = lam_ref[...]
    t1 = jnp.sum(p[:, 0, :] * p[:, 1, :], axis=-1, keepdims=True)
    t2 = jnp.sum(p[:, 2, :] * p[:, 3, :], axis=-1, keepdims=True)
    layer = lax.broadcasted_iota(jnp.int32, t1.shape, 0)
    init = jnp.zeros_like(t1)
    for l, v in enumerate(lam_inits):
        init = jnp.where(layer == l, v, init)
    o_ref[...] = jnp.broadcast_to(jnp.exp(t1) - jnp.exp(t2) + init, o_ref.shape)


def _diff_lambdas(lq1, lk1, lq2, lk2, lam_inits):
    depth = lq1.shape[0]
    params = jnp.stack([lq1, lk1, lq2, lk2], axis=1).astype(F32)
    out = pl.pallas_call(
        functools.partial(_lambda_kernel, lam_inits=lam_inits),
        out_shape=jax.ShapeDtypeStruct((depth, LANES), F32),
        name="diff_lambda",
    )(params)
    return out[:, 0]


def _rope(a, c, s_up, s_dn):
    outs = []
    for i in range(a.shape[1] // LANES):
        g = a[:, i * LANES:(i + 1) * LANES]
        outs.append(g * c + pltpu.roll(g, LANES - 16, 1) * s_up + pltpu.roll(g, 16, 1) * s_dn)
    return jnp.concatenate(outs, axis=1)


def _inproj_kernel(*refs, rope, blk, tn, ts):
    if rope:
        x_ref, mod_ref, w_ref, c_ref, su_ref, sd_ref, h_ref, z_ref, vt_ref = refs
    else:
        x_ref, mod_ref, w_ref, h_ref, z_ref = refs
    for sub in range(x_ref.shape[1] // ts):
        rows = slice(sub * ts, (sub + 1) * ts)
        xn = _layernorm(x_ref[0, rows, :])
        hb = (xn * (1.0 + mod_ref[0, 0, 1:2, :]) + mod_ref[0, 0, 0:1, :]).astype(BF16)
        h_ref[0, rows, :] = hb
        for j in range(w_ref.shape[1] // tn):
            cols = slice(j * tn, (j + 1) * tn)
            acc = _dot(hb, w_ref[:, cols])
            if rope and j in (blk["q_b"], blk["k_b"]):
                acc = _rope(acc, c_ref[rows, :], su_ref[rows, :], sd_ref[rows, :])
            if j in (blk["q_b"], blk["q_c"]):
                acc = acc * Q_SCALE
            z_ref[0, rows, cols] = acc.astype(z_ref.dtype)
            if rope and j == blk["v_b"]:
                vt_ref[0, :, rows] = acc.T.astype(vt_ref.dtype)


def _inproj(x, mod, mod_row0, w, blk, tn, z_dtype, rope_tabs=None):
    B, L, D = x.shape
    d_in = w.shape[1]
    tm = min(L, 512 if z_dtype == BF16 else 256)
    rope = rope_tabs is not None
    in_specs = [
        pl.BlockSpec((1, tm, D), lambda b, i: (b, i, 0)),
        pl.BlockSpec((1, 1, 3, D), lambda b, i: (mod_row0 + b, 0, 0, 0)),
        pl.BlockSpec((D, d_in), lambda b, i: (0, 0), pipeline_mode=pl.Buffered(1)),
    ]
    args = [x, mod, w]
    out_specs = [
        pl.BlockSpec((1, tm, D), lambda b, i: (b, i, 0)),
        pl.BlockSpec((1, tm, d_in), lambda b, i: (b, i, 0)),
    ]
    out_shape = [
        jax.ShapeDtypeStruct((B, L, D), BF16),
        jax.ShapeDtypeStruct((B, L, d_in), z_dtype),
    ]
    if rope:
        in_specs += [pl.BlockSpec((tm, LANES), lambda b, i: (i, 0))] * 3
        args += list(rope_tabs)
        out_specs.append(pl.BlockSpec((1, tn, tm), lambda b, i: (b, 0, i)))
        out_shape.append(jax.ShapeDtypeStruct((B, tn, L), BF16))
    return pl.pallas_call(
        functools.partial(_inproj_kernel, rope=rope, blk=blk, tn=tn, ts=tm // INPROJ_SUBTILES),
        grid=(B, L // tm),
        in_specs=in_specs,
        out_specs=out_specs,
        out_shape=out_shape,
        compiler_params=_cparams(("parallel", "parallel")),
        name="inproj_rope" if rope else "inproj",
    )(*args)


def _head_lanes(hh):
    lane = lax.broadcasted_iota(jnp.int32, (1, LANES), 1)
    return (lane < HEAD_DIM) if hh == 0 else (lane >= HEAD_DIM)


def _diff_head(q1, q2, chunks, hh, lam, lam_init, sg, g):
    tq = q1.shape[0]
    dv = 2 * HEAD_DIM
    qm = [jnp.where(_head_lanes(hh), q, jnp.zeros_like(q)).astype(BF16) for q in (q1, q2)]
    state = [(jnp.full((tq, 1), -jnp.inf, F32), jnp.zeros((tq, 2 * dv), F32)) for _ in range(2)]
    for k1c, k2c, vc in chunks:
        ones = jnp.ones((vc.shape[0], LANES), BF16)
        vaug = jnp.concatenate([vc[:, hh * dv:(hh + 1) * dv].astype(BF16), ones], axis=1)
        for m, kc in enumerate((k1c, k2c)):
            mx, acc = state[m]
            s = _dot_nt(qm[m], kc[...].astype(BF16))
            mx_new = jnp.maximum(mx, jnp.max(s, axis=-1, keepdims=True))
            p = jnp.exp2(s - mx_new).astype(BF16)
            state[m] = (mx_new, jnp.exp2(mx - mx_new) * acc + _dot(p, vaug))
    (_, a1), (_, a2) = state
    o = a1[:, :dv] * (1.0 / a1[:, dv:dv + 1]) - a2[:, :dv] * (lam / a2[:, dv:dv + 1])
    of = o * lax.rsqrt(jnp.mean(o * o, axis=-1, keepdims=True) + EPS)
    return of * sg * (1.0 - lam_init) * _silu(g.astype(F32))


def _diff_attn_kernel(lam_ref, q1_ref, q2_ref, k1_ref, k2_ref, vt_ref, g_ref, kc1_ref, kc2_ref, vct_ref, sg_ref,
                      o_ref, *, layer, lam_init, tq, tk, n_self, n_ctx):
    lam = lam_ref[layer]
    dv = 2 * HEAD_DIM
    ones = jnp.ones((DA_ONES_ROWS, tk), BF16)
    chunks = [(k1_ref.at[0, c * tk:(c + 1) * tk, :], k2_ref.at[0, c * tk:(c + 1) * tk, :],
               vt_ref.at[0, :, c * tk:(c + 1) * tk]) for c in range(n_self)]
    chunks += [(kc1_ref.at[0, 0, c * tk:(c + 1) * tk, :], kc2_ref.at[0, 0, c * tk:(c + 1) * tk, :],
                vct_ref.at[0, 0, :, c * tk:(c + 1) * tk]) for c in range(n_ctx)]
    units = [(sub, hh, m) for sub in range(q1_ref.shape[1] // tq) for hh in range(2) for m in range(2)]
    qm, state = {}, {}
    for u in units:
        sub, hh, m = u
        q = (q1_ref, q2_ref)[m][0, sub * tq:(sub + 1) * tq, :]
        qm[u] = jnp.where(_head_lanes(hh), q, jnp.zeros_like(q)).astype(BF16)
        state[u] = (jnp.full((1, tq), -jnp.inf, F32), jnp.zeros((dv + DA_ONES_ROWS, tq), F32))

    def scores(u, chunk):
        return _dot_nt(chunk[u[2]][...], qm[u])

    def update(u, chunk, st):
        hh = u[1]
        vt_aug = jnp.concatenate([chunk[2][hh * dv:(hh + 1) * dv, :], ones], axis=0)
        mx, acc = state[u]
        mx_new = jnp.maximum(mx, jnp.max(st, axis=0, keepdims=True))
        pt = jnp.exp2(st - mx_new).astype(BF16)
        state[u] = (mx_new, jnp.exp2(mx - mx_new) * acc + _dot(vt_aug, pt))

    cur = {u: scores(u, chunks[0]) for u in units}
    for c, chunk in enumerate(chunks):
        nxt = {}
        for u in units:
            if c + 1 < len(chunks):
                nxt[u] = scores(u, chunks[c + 1])
            update(u, chunk, cur[u])
        cur = nxt

    for sub in range(q1_ref.shape[1] // tq):
        rows = slice(sub * tq, (sub + 1) * tq)
        for hh in range(2):
            cols = slice(hh * dv, (hh + 1) * dv)
            a1, a2 = state[sub, hh, 0][1], state[sub, hh, 1][1]
            ot = a1[:dv] * (1.0 / a1[dv:dv + 1]) - a2[:dv] * (lam / a2[dv:dv + 1])
            oft = ot * lax.rsqrt(jnp.mean(ot * ot, axis=0, keepdims=True) + EPS)
            y = oft.T * sg_ref[...] * (1.0 - lam_init) * _silu(g_ref[0, rows, cols].astype(F32))
            o_ref[0, rows, cols] = y.astype(o_ref.dtype)


def _diff_attn(z, vt, blk, d_b, lam_vec, subln_g, layer, lam_init, cache):
    B, L, _ = z.shape
    tq = min(L, DA_TQ)
    bq = min(L, DA_SUBTILES * tq)
    tk = min(L, DA_TK)
    n_pairs = d_b // (4 * HEAD_DIM)
    col128 = lambda name: blk[name] * (d_b // LANES)
    col256 = lambda name: blk[name] * (d_b // (2 * LANES))
    qspec = lambda m: pl.BlockSpec((1, bq, LANES), lambda b, p, i: (b, i, col128("q_b") + m * n_pairs + p))
    kspec = lambda m: pl.BlockSpec((1, L, LANES), lambda b, p, i: (b, 0, col128("k_b") + m * n_pairs + p))
    ck, cvt = cache
    P = ck.shape[2]
    assert P % tk == 0 and L % tk == 0 and L % bq == 0
    in_specs = [
        pl.BlockSpec(memory_space=pltpu.SMEM),
        qspec(0), qspec(1), kspec(0), kspec(1),
        pl.BlockSpec((1, 2 * LANES, L), lambda b, p, i: (b, p, 0)),
        pl.BlockSpec((1, bq, 2 * LANES), lambda b, p, i: (b, i, col256("g_b") + p)),
        pl.BlockSpec((1, 1, P, LANES), lambda b, p, i: (b, layer, 0, p)),
        pl.BlockSpec((1, 1, P, LANES), lambda b, p, i: (b, layer, 0, n_pairs + p)),
        pl.BlockSpec((1, 1, 2 * LANES, P), lambda b, p, i: (b, layer, p, 0)),
        pl.BlockSpec((1, 2 * HEAD_DIM), lambda b, p, i: (0, 0)),
    ]
    return pl.pallas_call(
        functools.partial(_diff_attn_kernel, layer=layer, lam_init=lam_init, tq=tq, tk=tk, n_self=L // tk,
                          n_ctx=P // tk),
        grid=(B, n_pairs, L // bq),
        in_specs=in_specs,
        out_specs=pl.BlockSpec((1, bq, 2 * LANES), lambda b, p, i: (b, i, p)),
        out_shape=jax.ShapeDtypeStruct((B, L, d_b), BF16),
        compiler_params=_cparams(("parallel", "parallel", "parallel")),
        name="diff_attn_latent",
    )(lam_vec, z, z, z, z, vt, z, ck, ck, cvt, subln_g)


def _ctx_attn_kernel(lam_ref, qb_ref, kb_ref, vb_ref, gb_ref, qc_ref, kc_ref, vc_ref, gc_ref, sg_ref,
                     yb_ref, yc_ref, *, layer, lam_init):
    dv = 2 * HEAD_DIM
    n_pairs_b = qb_ref.shape[2] // (2 * LANES)
    for p in range(n_pairs_b):
        blk1 = slice(p * LANES, (p + 1) * LANES)
        blk2 = slice((n_pairs_b + p) * LANES, (n_pairs_b + p + 1) * LANES)
        vcols = slice(p * 2 * dv, (p + 1) * 2 * dv)
        chunks = [(kb_ref.at[0, :, blk1], kb_ref.at[0, :, blk2], vb_ref.at[0, :, vcols])]
        for hh in range(2):
            cols = slice((2 * p + hh) * dv, (2 * p + hh + 1) * dv)
            y = _diff_head(qb_ref[0, :, blk1], qb_ref[0, :, blk2], chunks, hh, lam_ref[layer], lam_init,
                           sg_ref[...], gb_ref[0, :, cols])
            yb_ref[0, :, cols] = y.astype(yb_ref.dtype)

    lo = _head_lanes(0)
    for p in range(qc_ref.shape[2] // LANES):
        cols = slice(p * LANES, (p + 1) * LANES)
        q = qc_ref[0, :, cols]
        k = kc_ref[0, :, cols].astype(BF16)
        vaug = jnp.concatenate([vc_ref[0, :, cols].astype(BF16), jnp.ones((k.shape[0], LANES), BF16)], axis=1)
        outs = []
        for hh in range(2):
            qm = jnp.where(_head_lanes(hh), q, jnp.zeros_like(q)).astype(BF16)
            s = _dot_nt(qm, k)
            o = _dot(jnp.exp2(s - jnp.max(s, axis=-1, keepdims=True)).astype(BF16), vaug)
            outs.append(o[:, :LANES] * (1.0 / o[:, LANES:LANES + 1]))
        o = jnp.where(lo, outs[0], outs[1])
        yc_ref[0, :, cols] = (o * _silu(gc_ref[0, :, cols].astype(F32))).astype(yc_ref.dtype)


def _ctx_attn(z, blk, d_b, d_c, lam_vec, subln_g, layer, lam_init):
    B, L, _ = z.shape
    assert d_b == d_c
    spec = lambda name: pl.BlockSpec((1, L, d_b), lambda b, c=blk[name]: (b, 0, c))
    names = ("q_b", "k_b", "v_b", "g_b", "q_c", "k_c", "v_c", "g_c")
    return pl.pallas_call(
        functools.partial(_ctx_attn_kernel, layer=layer, lam_init=lam_init),
        grid=(B,),
        in_specs=[pl.BlockSpec(memory_space=pltpu.SMEM), *[spec(n) for n in names],
                  pl.BlockSpec((1, 2 * HEAD_DIM), lambda b: (0, 0))],
        out_specs=[pl.BlockSpec((1, L, d_b), lambda b: (b, 0, 0)), pl.BlockSpec((1, L, d_c), lambda b: (b, 0, 0))],
        out_shape=[jax.ShapeDtypeStruct((B, L, d_b), BF16), jax.ShapeDtypeStruct((B, L, d_c), BF16)],
        compiler_params=_cparams(("parallel",)),
        name="ctx_attn",
    )(lam_vec, *([z] * len(names)), subln_g)


def _na_kernel(q_ref, k_ref, v_ref, g_ref, kc_ref, vc_ref, bias_ref, o_ref, *, rows):
    lane = lax.broadcasted_iota(jnp.int32, (1, LANES), 1)
    n_loc = WIN_R * GRID_W
    P = kc_ref.shape[2]
    kc = kc_ref[0, 0]
    vc_aug = jnp.concatenate([vc_ref[0, 0], jnp.ones((P, LANES), BF16)], axis=1)
    ones_loc = jnp.ones((n_loc, LANES), BF16)
    tq = NA_ROWS_PER_BLOCK * GRID_W
    lo = lane < HEAD_DIM

    def block(i, carry):
        q0 = pl.multiple_of(i * tq, tq)
        q = q_ref[0, pl.ds(q0, tq), :]
        zero = jnp.zeros_like(q)
        q_heads = (jnp.where(lo, q, zero), jnp.where(lo, zero, q))
        q2 = jnp.concatenate([qh[j * GRID_W:(j + 1) * GRID_W] for j in range(NA_ROWS_PER_BLOCK) for qh in q_heads],
                             axis=0)
        s_ctx = _dot_nt(q2, kc)
        m_ctx = jnp.max(s_ctx, axis=-1, keepdims=True)
        o_rows, p_ctx_rows = [], []
        for j in range(NA_ROWS_PER_BLOCK):
            r = i * NA_ROWS_PER_BLOCK + j
            rs = jnp.clip(r - WIN_R // 2, 0, rows - WIN_R)
            off = rs - r + WIN_R - 1
            k0 = pl.multiple_of(rs * GRID_W, GRID_W)
            kl = k_ref[0, pl.ds(k0, n_loc), :]
            vl_aug = jnp.concatenate([v_ref[0, pl.ds(k0, n_loc), :], ones_loc], axis=1)
            rsl = slice(2 * j * GRID_W, 2 * (j + 1) * GRID_W)
            s_loc = _dot_nt(q2[rsl], kl) + bias_ref[0, :, off].reshape(2 * GRID_W, n_loc)
            m = jnp.maximum(jnp.max(s_loc, axis=-1, keepdims=True), m_ctx[rsl])
            p_ctx_rows.append(jnp.exp2(s_ctx[rsl] - m).astype(BF16))
            o_rows.append(_dot(jnp.exp2(s_loc - m).astype(BF16), vl_aug))
        o = jnp.concatenate(o_rows, axis=0) + _dot(jnp.concatenate(p_ctx_rows, axis=0), vc_aug)
        o = o[:, :LANES] * (1.0 / o[:, LANES:LANES + 1])
        o = jnp.concatenate(
            [jnp.where(lo, o[2 * j * GRID_W:(2 * j + 1) * GRID_W], o[(2 * j + 1) * GRID_W:(2 * j + 2) * GRID_W])
             for j in range(NA_ROWS_PER_BLOCK)], axis=0)
        y = o * _silu(g_ref[0, pl.ds(q0, tq), :].astype(F32))
        o_ref[0, pl.ds(q0, tq), :] = y.astype(o_ref.dtype)
        return carry

    n_blocks = rows // NA_ROWS_PER_BLOCK
    lax.fori_loop(0, n_blocks, block, 0, unroll=math.gcd(n_blocks, NA_BLOCK_UNROLL))


def _na_attn(z, blk, d_c, nk, nv, bias_tab, layer):
    B, L, _ = z.shape
    rows = L // GRID_W
    assert rows >= WIN_R and L % GRID_W == 0 and rows % NA_ROWS_PER_BLOCK == 0
    P = nk.shape[2]
    n_pairs = d_c // LANES
    spec = lambda name: pl.BlockSpec((1, L, LANES), lambda b, p, c=blk[name] * n_pairs: (b, 0, c + p))
    cspec = pl.BlockSpec((1, 1, P, LANES), lambda b, p: (b, layer, 0, p))
    return pl.pallas_call(
        functools.partial(_na_kernel, rows=rows),
        grid=(B, n_pairs),
        in_specs=[
            spec("q_c"), spec("k_c"), spec("v_c"), spec("g_c"), cspec, cspec,
            pl.BlockSpec((1, 2, WIN_R, GRID_W, WIN_R * GRID_W), lambda b, p: (layer, p, 0, 0, 0)),
        ],
        out_specs=pl.BlockSpec((1, L, LANES), lambda b, p: (b, 0, p)),
        out_shape=jax.ShapeDtypeStruct((B, L, d_c), BF16),
        compiler_params=_cparams(("parallel", "parallel")),
        name="na_attn_latent",
    )(z, z, z, z, nk, nv, bias_tab)


def _spatial_gate(u, v, g, ng_ref, nb_ref, ws_ref, bs_ref):
    n_groups = v.shape[1] // SG_GROUP_W
    vn = (_layernorm(v.astype(F32)) * ng_ref[...] + nb_ref[...]).astype(BF16)
    ys = []
    for n in range(v.shape[0] // CHUNK):
        rows = slice(n * CHUNK, (n + 1) * CHUNK)
        sv = jnp.concatenate(
            [_dot(ws_ref[gi], vn[rows, gi * SG_GROUP_W:(gi + 1) * SG_GROUP_W]) for gi in range(n_groups)],
            axis=1) + bs_ref[...]
        ys.append((u[rows].astype(F32) * sv * _silu(g[rows].astype(F32))).astype(BF16))
    return jnp.concatenate(ys, axis=0)


def _merge_kernel(x_ref, h_ref, u_ref, v_ref, g_ref, yb_ref, yc_ref, mod_ref, ng_ref, nb_ref, ws_ref, bs_ref,
                  wmg_ref, bmg_ref, wa_ref, wb_ref, wc_ref, wo_ref, lg_ref, lb_ref, o_ref, *, alpha, ts):
    D = x_ref.shape[2]
    for sub in range(x_ref.shape[1] // ts):
        rows = slice(sub * ts, (sub + 1) * ts)
        h = h_ref[0, rows, :]
        ya = _spatial_gate(u_ref[0, rows, :], v_ref[0, rows, :], g_ref[0, rows, :], ng_ref, nb_ref, ws_ref, bs_ref)
        m = None
        for i, (y, w_ref) in enumerate(((ya, wa_ref), (yb_ref[0, rows, :], wb_ref), (yc_ref[0, rows, :], wc_ref))):
            gate = _sigmoid(_dot(h, wmg_ref[:, i * D:(i + 1) * D]) + bmg_ref[:, i * D:(i + 1) * D])
            t = gate * _dot(y, w_ref[...])
            m = t if m is None else m + t
        out = _dot(m.astype(BF16), wo_ref[...])
        t = alpha * x_ref[0, rows, :] + mod_ref[0, 0, 2:3, :] * out
        o_ref[0, rows, :] = _layernorm(t) * lg_ref[...] + lb_ref[...]


def _merge(x, h, z, blk, yb, yc, mod, mod_row0, sg, wmg, bmg, wa, wb, wc, wo, lg, lb, alpha):
    B, L, D = x.shape
    d_a = wa.shape[0]
    tm = min(L, MERGE_TM)
    ts = tm // MERGE_SUBTILES
    assert ts % CHUNK == 0 and L % tm == 0
    tok = lambda w: pl.BlockSpec((1, tm, w), lambda b, i: (b, i, 0))
    zspec = lambda name: pl.BlockSpec((1, tm, d_a), lambda b, i, c=blk[name]: (b, i, c))
    const = lambda a: pl.BlockSpec(a.shape, lambda b, i: (0,) * a.ndim, pipeline_mode=pl.Buffered(1))
    weights = (*sg, wmg, bmg, wa, wb, wc, wo, lg, lb)
    return pl.pallas_call(
        functools.partial(_merge_kernel, alpha=alpha, ts=ts),
        grid=(B, L // tm),
        in_specs=[
            tok(D), tok(D), zspec("u_a"), zspec("v_a"), zspec("g_a"), tok(yb.shape[2]), tok(yc.shape[2]),
            pl.BlockSpec((1, 1, 3, D), lambda b, i: (mod_row0 + b, 0, 0, 0)),
            *[const(w) for w in weights],
        ],
        out_specs=tok(D),
        out_shape=jax.ShapeDtypeStruct((B, L, D), F32),
        compiler_params=_cparams(("parallel", "parallel")),
        name="merge_postnorm",
    )(x, h, z, z, z, yb, yc, mod, *weights)


def _rope_tables(n_tok):
    half = HEAD_DIM // 2
    quarter = HEAD_DIM // 4
    t = jnp.arange(n_tok)
    inv = 1.0 / (ROPE_THETA ** (jnp.arange(0, half, 2, dtype=F32) / half))
    ang = jnp.stack([(t // GRID_W).astype(F32)[:, None] * inv, (t % GRID_W).astype(F32)[:, None] * inv], axis=1)
    cos = jnp.cos(ang)[:, :, None, :]
    sin = jnp.sin(ang)[:, :, None, :]
    zero = jnp.zeros_like(sin)
    shape = (n_tok, 2, 2, quarter)
    c = jnp.broadcast_to(cos, shape).reshape(n_tok, HEAD_DIM)
    s_up = jnp.concatenate([-sin, zero], axis=2).reshape(n_tok, HEAD_DIM)
    s_dn = jnp.concatenate([zero, sin], axis=2).reshape(n_tok, HEAD_DIM)
    rep = LANES // HEAD_DIM
    return tuple(jnp.tile(a, (1, rep)) for a in (c, s_up, s_dn))


def _na_bias_kernel(rb_ref, o_ref):
    shape = (GRID_W, LANES)
    qcol = lax.broadcasted_iota(jnp.int32, shape, 0)
    lane = lax.broadcasted_iota(jnp.int32, shape, 1)
    kcol = lane & (GRID_W - 1)
    cstart = jnp.clip(qcol - WIN_W // 2, 0, GRID_W - WIN_W)
    in_window = (kcol >= cstart) & (kcol < cstart + WIN_W)

    def toeplitz(y, lane0):
        row = jnp.broadcast_to(rb_ref[0, y:y + 1, :], shape)
        return pltpu.roll(row, (lane0 - (WIN_W - 1)) % LANES, 1, stride=1, stride_axis=0)

    n_dy = 2 * WIN_R - 1
    t_lo = [toeplitz(y, 0) for y in range(n_dy)]
    t_hi = [toeplitz(y, GRID_W) for y in range(n_dy)]
    for off in range(WIN_R):
        for jp in range(WIN_R // 2):
            y = off + 2 * jp
            tile = jnp.where(lane < GRID_W, t_lo[y], t_hi[y + 1]) * LOG2E
            o_ref[0, 0, off, :, jp * LANES:(jp + 1) * LANES] = jnp.where(in_window, tile, NEG_INF)


def _na_bias_tables(rel_bias):
    depth, n_heads, n_dy, n_dx = rel_bias.shape
    assert n_dy == 2 * WIN_R - 1 and n_dx == 2 * WIN_W - 1 and WIN_R % 2 == 0
    rb = jnp.pad(rel_bias.astype(F32), ((0, 0), (0, 0), (0, 16 - n_dy), (0, LANES - n_dx)))
    return pl.pallas_call(
        _na_bias_kernel,
        grid=(depth, n_heads),
        in_specs=[pl.BlockSpec((1, 16, LANES), lambda l, h: (l * n_heads + h, 0, 0))],
        out_specs=pl.BlockSpec((1, 1, WIN_R, GRID_W, WIN_R * GRID_W), lambda l, h: (l, h, 0, 0, 0)),
        out_shape=jax.ShapeDtypeStruct((depth, n_heads, WIN_R, GRID_W, WIN_R * GRID_W), F32),
        compiler_params=_cparams(("parallel", "parallel")),
        name="na_bias_table",
    )(rb.reshape(depth * n_heads, 16, LANES))


def kernel(x_prompt, x_sample, c, cache_diff_k, cache_diff_v, cache_na_k, cache_na_v, c_ctx, w_ada, b_ada, w_in, sg_norm_g, sg_norm_b, w_spatial, b_spatial, lambda_q1, lambda_k1, lambda_q2, lambda_k2, diff_subln_g, na_rel_bias, w_br_a, w_br_b, w_br_c, w_mgate, b_mgate, w_out, ln_g, ln_b):
    depth, D, d_in = w_in.shape
    batch, seq, _ = x_prompt.shape
    dec_batch, dec_seq, _ = x_sample.shape
    past = cache_diff_k.shape[4]
    d_a, d_b, d_c = w_br_a.shape[1], w_br_b.shape[1], w_br_c.shape[1]
    h_b = d_b // (2 * HEAD_DIM)
    h_c = d_c // HEAD_DIM
    tn = d_a
    assert d_a == d_b == d_c and d_in == 11 * tn and tn % (2 * LANES) == 0
    assert d_a // SG_GROUP_W == w_spatial.shape[1] and w_spatial.shape[2] == CHUNK
    names = ("u_a", "v_a", "g_a", "q_b", "k_b", "v_b", "g_b", "q_c", "k_c", "v_c", "g_c")
    blk = {n: i for i, n in enumerate(names)}
    alpha = (2 * depth) ** 0.25
    lam_inits = tuple(0.8 - 0.6 * math.exp(-0.3 * l) for l in range(depth))

    mod_rows = -(-(1 + dec_batch) // 8) * 8
    cvec = jnp.zeros((mod_rows, D), F32).at[0].set(c_ctx).at[1:1 + dec_batch].set(c)
    mod = _ada_mod(cvec, w_ada, b_ada).reshape(depth * mod_rows, 1, 3, D)
    lam_vec = _diff_lambdas(lambda_q1, lambda_k1, lambda_q2, lambda_k2, lam_inits)

    w_in_b = w_in.astype(BF16)
    w_s_b = w_spatial.astype(BF16)
    b_s_full = jnp.repeat(jnp.swapaxes(b_spatial, 1, 2), SG_GROUP_W, axis=2)
    wmg_b, wa_b, wb_b, wc_b, wo_b = (w.astype(BF16) for w in (w_mgate, w_br_a, w_br_b, w_br_c, w_out))
    ck = cache_diff_k.transpose(0, 1, 4, 2, 3, 5).reshape(dec_batch, depth, past, d_b).astype(BF16)
    cvt = cache_diff_v.transpose(0, 1, 2, 4, 3).reshape(dec_batch, depth, d_b, past).astype(BF16)
    nk = cache_na_k.transpose(0, 1, 3, 2, 4).reshape(dec_batch, depth, past, d_c).astype(BF16)
    nv = cache_na_v.transpose(0, 1, 3, 2, 4).reshape(dec_batch, depth, past, d_c).astype(BF16)
    rope_tabs = _rope_tables(dec_seq)
    bias_tab = _na_bias_tables(na_rel_bias)

    xp = x_prompt.reshape(1, batch * seq, D)
    xs = x_sample
    new_dk, new_dv, new_nk, new_nv = [], [], [], []
    for l in range(depth):
        row2 = lambda a: a[l].reshape(1, -1)
        merge_w = (wmg_b[l], row2(b_mgate), wa_b[l], wb_b[l], wc_b[l], wo_b[l], row2(ln_g), row2(ln_b))
        sg = (row2(sg_norm_g), row2(sg_norm_b), w_s_b[l], b_s_full[l])

        h, z = _inproj(xp, mod, l * mod_rows, w_in_b[l], blk, tn, F32)
        zc = z.reshape(batch, seq, d_in)
        yb, yc = _ctx_attn(zc, blk, d_b, d_c, lam_vec, row2(diff_subln_g), l, lam_inits[l])
        flat = lambda a: a.reshape(1, batch * seq, a.shape[-1])
        xp = _merge(xp, h, z, blk, flat(yb), flat(yc), mod, l * mod_rows, sg, *merge_w, alpha)
        col = lambda n: zc[:, :, blk[n] * tn:(blk[n] + 1) * tn]
        new_dk.append(col("k_b").reshape(batch, seq, 2, h_b, HEAD_DIM).transpose(0, 2, 3, 1, 4))
        new_dv.append(col("v_b").reshape(batch, seq, h_b, 2 * HEAD_DIM).transpose(0, 2, 1, 3))
        new_nk.append(col("k_c").reshape(batch, seq, h_c, HEAD_DIM).transpose(0, 2, 1, 3))
        new_nv.append(col("v_c").reshape(batch, seq, h_c, HEAD_DIM).transpose(0, 2, 1, 3))

        h, z, vt = _inproj(xs, mod, l * mod_rows + 1, w_in_b[l], blk, tn, BF16, rope_tabs)
        yb = _diff_attn(z, vt, blk, d_b, lam_vec, row2(diff_subln_g), l, lam_inits[l], cache=(ck, cvt))
        yc = _na_attn(z, blk, d_c, nk, nv, bias_tab, l)
        xs = _merge(xs, h, z, blk, yb, yc, mod, l * mod_rows + 1, sg, *merge_w, alpha)

    return (xp.reshape(batch, seq, D), xs,
            jnp.stack(new_dk, axis=1), jnp.stack(new_dv, axis=1),
            jnp.stack(new_nk, axis=1), jnp.stack(new_nv, axis=1))
```

```python
import functools
import math

import jax
import jax.numpy as jnp
from jax import lax
from jax.experimental import pallas as pl
from jax.experimental.pallas import tpu as pltpu

GRID_W = 64
CHUNK = 128
HEAD_DIM = 64
SG_GROUP_W = 128
WIN_R = 8
WIN_W = 16
DA_TQ = 256
DA_TK = 256
DA_SUBTILES = 2
DA_ONES_ROWS = 16
INPROJ_SUBTILES = 2
MERGE_TM = 512
MERGE_SUBTILES = 2
NA_ROWS_PER_BLOCK = 4
NA_BLOCK_UNROLL = 8
ROPE_THETA = 10000.0
EPS = 1e-6
NEG_INF = -1e30
LANES = 128
LOG2E = math.log2(math.e)
Q_SCALE = HEAD_DIM ** -0.5 * LOG2E

BF16 = jnp.bfloat16
F32 = jnp.float32

VMEM_LIMIT = 56 * 1024 * 1024


def _cparams(sem):
    return pltpu.CompilerParams(dimension_semantics=sem, vmem_limit_bytes=VMEM_LIMIT)


def _sigmoid(x):
    return 1.0 / (1.0 + jnp.exp(-x))


def _silu(x):
    return x * _sigmoid(x)


def _dot(a, b):
    return jnp.dot(a, b, preferred_element_type=F32)


def _dot_nt(a, b):
    return lax.dot_general(a, b, (((1,), (1,)), ((), ())), preferred_element_type=F32)


def _layernorm(x):
    mu = jnp.mean(x, axis=-1, keepdims=True)
    xc = x - mu
    var = jnp.mean(xc * xc, axis=-1, keepdims=True)
    return xc * lax.rsqrt(var + EPS)


def _ada_kernel(c_ref, w_ref, b_ref, o_ref):
    s = _silu(c_ref[...]).astype(BF16)
    o_ref[0] = _dot(s, w_ref[0].astype(BF16)) + b_ref[0]


def _ada_mod(cvec, w_ada, b_ada):
    depth, d, d3 = w_ada.shape
    rows = cvec.shape[0]
    tn = 1024
    return pl.pallas_call(
        _ada_kernel,
        grid=(depth, d3 // tn),
        in_specs=[
            pl.BlockSpec((rows, d), lambda l, j: (0, 0)),
            pl.BlockSpec((1, d, tn), lambda l, j: (l, 0, j)),
            pl.BlockSpec((1, 1, tn), lambda l, j: (l, 0, j)),
        ],
        out_specs=pl.BlockSpec((1, rows, tn), lambda l, j: (l, 0, j)),
        out_shape=jax.ShapeDtypeStruct((depth, rows, d3), F32),
        compiler_params=_cparams(("parallel", "parallel")),
        name="ada_mod",
    )(cvec, w_ada, b_ada.reshape(depth, 1, d3))


def _lambda_kernel(lam_ref, o_ref, *, lam_inits):
    p = lam_ref[...]
    t1 = jnp.sum(p[:, 0, :] * p[:, 1, :], axis=-1, keepdims=True)
    t2 = jnp.sum(p[:, 2, :] * p[:, 3, :], axis=-1, keepdims=True)
    layer = lax.broadcasted_iota(jnp.int32, t1.shape, 0)
    init = jnp.zeros_like(t1)
    for l, v in enumerate(lam_inits):
        init = jnp.where(layer == l, v, init)
    o_ref[...] = jnp.broadcast_to(jnp.exp(t1) - jnp.exp(t2) + init, o_ref.shape)


def _diff_lambdas(lq1, lk1, lq2, lk2, lam_inits):
    depth = lq1.shape[0]
    params = jnp.stack([lq1, lk1, lq2, lk2], axis=1).astype(F32)
    out = pl.pallas_call(
        functools.partial(_lambda_kernel, lam_inits=lam_inits),
        out_shape=jax.ShapeDtypeStruct((depth, LANES), F32),
        name="diff_lambda",
    )(params)
    return out[:, 0]


def _rope(a, c, s_up, s_dn):
    outs = []
    for i in range(a.shape[1] // LANES):
        g = a[:, i * LANES:(i + 1) * LANES]
        outs.append(g * c + pltpu.roll(g, LANES - 16, 1) * s_up + pltpu.roll(g, 16, 1) * s_dn)
    return jnp.concatenate(outs, axis=1)


def _inproj_kernel(*refs, rope, blk, tn, ts):
    if rope:
        x_ref, mod_ref, w_ref, c_ref, su_ref, sd_ref, h_ref, z_ref, vt_ref = refs
    else:
        x_ref, mod_ref, w_ref, h_ref, z_ref = refs
    for sub in range(x_ref.shape[1] // ts):
        rows = slice(sub * ts, (sub + 1) * ts)
        xn = _layernorm(x_ref[0, rows, :])
        hb = (xn * (1.0 + mod_ref[0, 0, 1:2, :]) + mod_ref[0, 0, 0:1, :]).astype(BF16)
        h_ref[0, rows, :] = hb
        for j in range(w_ref.shape[1] // tn):
            cols = slice(j * tn, (j + 1) * tn)
            acc = _dot(hb, w_ref[:, cols])
            if rope and j in (blk["q_b"], blk["k_b"]):
                acc = _rope(acc, c_ref[rows, :], su_ref[rows, :], sd_ref[rows, :])
            if j in (blk["q_b"], blk["q_c"]):
                acc = acc * Q_SCALE
            z_ref[0, rows, cols] = acc.astype(z_ref.dtype)
            if rope and j == blk["v_b"]:
                vt_ref[0, :, rows] = acc.T.astype(vt_ref.dtype)


def _inproj(x, mod, mod_row0, w, blk, tn, z_dtype, rope_tabs=None):
    B, L, D = x.shape
    d_in = w.shape[1]
    tm = min(L, 512 if z_dtype == BF16 else 256)
    rope = rope_tabs is not None
    in_specs = [
        pl.BlockSpec((1, tm, D), lambda b, i: (b, i, 0)),
        pl.BlockSpec((1, 1, 3, D), lambda b, i: (mod_row0 + b, 0, 0, 0)),
        pl.BlockSpec((D, d_in), lambda b, i: (0, 0), pipeline_mode=pl.Buffered(1)),
    ]
    args = [x, mod, w]
    out_specs = [
        pl.BlockSpec((1, tm, D), lambda b, i: (b, i, 0)),
        pl.BlockSpec((1, tm, d_in), lambda b, i: (b, i, 0)),
    ]
    out_shape = [
        jax.ShapeDtypeStruct((B, L, D), BF16),
        jax.ShapeDtypeStruct((B, L, d_in), z_dtype),
    ]
    if rope:
        in_specs += [pl.BlockSpec((tm, LANES), lambda b, i: (i, 0))] * 3
        args += list(rope_tabs)
        out_specs.append(pl.BlockSpec((1, tn, tm), lambda b, i: (b, 0, i)))
        out_shape.append(jax.ShapeDtypeStruct((B, tn, L), BF16))
    return pl.pallas_call(
        functools.partial(_inproj_kernel, rope=rope, blk=blk, tn=tn, ts=tm // INPROJ_SUBTILES),
        grid=(B, L // tm),
        in_specs=in_specs,
        out_specs=out_specs,
        out_shape=out_shape,
        compiler_params=_cparams(("parallel", "parallel")),
        name="inproj_rope" if rope else "inproj",
    )(*args)


def _head_lanes(hh):
    lane = lax.broadcasted_iota(jnp.int32, (1, LANES), 1)
    return (lane < HEAD_DIM) if hh == 0 else (lane >= HEAD_DIM)


def _diff_head(q1, q2, chunks, hh, lam, lam_init, sg, g):
    tq = q1.shape[0]
    dv = 2 * HEAD_DIM
    qm = [jnp.where(_head_lanes(hh), q, jnp.zeros_like(q)).astype(BF16) for q in (q1, q2)]
    state = [(jnp.full((tq, 1), -jnp.inf, F32), jnp.zeros((tq, 2 * dv), F32)) for _ in range(2)]
    for k1c, k2c, vc in chunks:
        ones = jnp.ones((vc.shape[0], LANES), BF16)
        vaug = jnp.concatenate([vc[:, hh * dv:(hh + 1) * dv].astype(BF16), ones], axis=1)
        for m, kc in enumerate((k1c, k2c)):
            mx, acc = state[m]
            s = _dot_nt(qm[m], kc[...].astype(BF16))
            mx_new = jnp.maximum(mx, jnp.max(s, axis=-1, keepdims=True))
            p = jnp.exp2(s - mx_new).astype(BF16)
            state[m] = (mx_new, jnp.exp2(mx - mx_new) * acc + _dot(p, vaug))
    (_, a1), (_, a2) = state
    o = a1[:, :dv] * (1.0 / a1[:, dv:dv + 1]) - a2[:, :dv] * (lam / a2[:, dv:dv + 1])
    of = o * lax.rsqrt(jnp.mean(o * o, axis=-1, keepdims=True) + EPS)
    return of * sg * (1.0 - lam_init) * _silu(g.astype(F32))


def _pair_lanes(ref):
    return jnp.concatenate([ref[0], ref[1]], axis=-1).astype(BF16)


def _diff_attn_kernel(lam_ref, q1_ref, q2_ref, k1_ref, k2_ref, vt_ref, g_ref, kc1_ref, kc2_ref, vc_ref, sg_ref,
                      o_ref, *, layer, lam_init, tq, tk, n_self, n_ctx):
    lam = lam_ref[layer]
    dv = 2 * HEAD_DIM
    ones = jnp.ones((DA_ONES_ROWS, tk), BF16)
    kc = [_pair_lanes(r.at[0, 0, 0]) for r in (kc1_ref, kc2_ref)]
    vct = jnp.concatenate([vc_ref[0, 0, hd].T for hd in range(2)], axis=0).astype(BF16)
    chunks = [(k1_ref.at[0, c * tk:(c + 1) * tk, :], k2_ref.at[0, c * tk:(c + 1) * tk, :],
               vt_ref.at[0, :, c * tk:(c + 1) * tk]) for c in range(n_self)]
    chunks += [(kc[0][c * tk:(c + 1) * tk], kc[1][c * tk:(c + 1) * tk], vct[:, c * tk:(c + 1) * tk])
               for c in range(n_ctx)]
    units = [(sub, hh, m) for sub in range(q1_ref.shape[1] // tq) for hh in range(2) for m in range(2)]
    qm, state = {}, {}
    for u in units:
        sub, hh, m = u
        q = (q1_ref, q2_ref)[m][0, sub * tq:(sub + 1) * tq, :]
        qm[u] = jnp.where(_head_lanes(hh), q, jnp.zeros_like(q)).astype(BF16)
        state[u] = (jnp.full((1, tq), -jnp.inf, F32), jnp.zeros((dv + DA_ONES_ROWS, tq), F32))

    def scores(u, chunk):
        return _dot_nt(chunk[u[2]][...], qm[u])

    def update(u, chunk, st):
        hh = u[1]
        vt_aug = jnp.concatenate([chunk[2][hh * dv:(hh + 1) * dv, :], ones], axis=0)
        mx, acc = state[u]
        mx_new = jnp.maximum(mx, jnp.max(st, axis=0, keepdims=True))
        pt = jnp.exp2(st - mx_new).astype(BF16)
        state[u] = (mx_new, jnp.exp2(mx - mx_new) * acc + _dot(vt_aug, pt))

    cur = {u: scores(u, chunks[0]) for u in units}
    for c, chunk in enumerate(chunks):
        nxt = {}
        for u in units:
            if c + 1 < len(chunks):
                nxt[u] = scores(u, chunks[c + 1])
            update(u, chunk, cur[u])
        cur = nxt

    for sub in range(q1_ref.shape[1] // tq):
        rows = slice(sub * tq, (sub + 1) * tq)
        for hh in range(2):
            cols = slice(hh * dv, (hh + 1) * dv)
            a1, a2 = state[sub, hh, 0][1], state[sub, hh, 1][1]
            ot = a1[:dv] * (1.0 / a1[dv:dv + 1]) - a2[:dv] * (lam / a2[dv:dv + 1])
            oft = ot * lax.rsqrt(jnp.mean(ot * ot, axis=0, keepdims=True) + EPS)
            y = oft.T * sg_ref[...] * (1.0 - lam_init) * _silu(g_ref[0, rows, cols].astype(F32))
            o_ref[0, rows, cols] = y.astype(o_ref.dtype)


def _diff_attn(z, vt, blk, d_b, lam_vec, subln_g, layer, lam_init, cache):
    B, L, _ = z.shape
    tq = min(L, DA_TQ)
    bq = min(L, DA_SUBTILES * tq)
    tk = min(L, DA_TK)
    n_pairs = d_b // (4 * HEAD_DIM)
    col128 = lambda name: blk[name] * (d_b // LANES)
    col256 = lambda name: blk[name] * (d_b // (2 * LANES))
    qspec = lambda m: pl.BlockSpec((1, bq, LANES), lambda b, p, i: (b, i, col128("q_b") + m * n_pairs + p))
    kspec = lambda m: pl.BlockSpec((1, L, LANES), lambda b, p, i: (b, 0, col128("k_b") + m * n_pairs + p))
    ck, cv = cache
    P = ck.shape[4]
    assert P % tk == 0 and L % tk == 0 and L % bq == 0
    kcspec = lambda m: pl.BlockSpec((1, 1, 1, 2, P, HEAD_DIM), lambda b, p, i: (b, layer, m, p, 0, 0))
    in_specs = [
        pl.BlockSpec(memory_space=pltpu.SMEM),
        qspec(0), qspec(1), kspec(0), kspec(1),
        pl.BlockSpec((1, 2 * LANES, L), lambda b, p, i: (b, p, 0)),
        pl.BlockSpec((1, bq, 2 * LANES), lambda b, p, i: (b, i, col256("g_b") + p)),
        kcspec(0), kcspec(1),
        pl.BlockSpec((1, 1, 2, P, 2 * HEAD_DIM), lambda b, p, i: (b, layer, p, 0, 0)),
        pl.BlockSpec((1, 2 * HEAD_DIM), lambda b, p, i: (0, 0)),
    ]
    return pl.pallas_call(
        functools.partial(_diff_attn_kernel, layer=layer, lam_init=lam_init, tq=tq, tk=tk, n_self=L // tk,
                          n_ctx=P // tk),
        grid=(B, n_pairs, L // bq),
        in_specs=in_specs,
        out_specs=pl.BlockSpec((1, bq, 2 * LANES), lambda b, p, i: (b, i, p)),
        out_shape=jax.ShapeDtypeStruct((B, L, d_b), BF16),
        compiler_params=_cparams(("parallel", "parallel", "parallel")),
        name="diff_attn_latent",
    )(lam_vec, z, z, z, z, vt, z, ck, ck, cv, subln_g)


def _ctx_attn_kernel(lam_ref, qb_ref, kb_ref, vb_ref, gb_ref, qc_ref, kc_ref, vc_ref, gc_ref, sg_ref,
                     yb_ref, yc_ref, *, layer, lam_init):
    dv = 2 * HEAD_DIM
    n_pairs_b = qb_ref.shape[2] // (2 * LANES)
    for p in range(n_pairs_b):
        blk1 = slice(p * LANES, (p + 1) * LANES)
        blk2 = slice((n_pairs_b + p) * LANES, (n_pairs_b + p + 1) * LANES)
        vcols = slice(p * 2 * dv, (p + 1) * 2 * dv)
        chunks = [(kb_ref.at[0, :, blk1], kb_ref.at[0, :, blk2], vb_ref.at[0, :, vcols])]
        for hh in range(2):
            cols = slice((2 * p + hh) * dv, (2 * p + hh + 1) * dv)
            y = _diff_head(qb_ref[0, :, blk1], qb_ref[0, :, blk2], chunks, hh, lam_ref[layer], lam_init,
                           sg_ref[...], gb_ref[0, :, cols])
            yb_ref[0, :, cols] = y.astype(yb_ref.dtype)

    lo = _head_lanes(0)
    for p in range(qc_ref.shape[2] // LANES):
        cols = slice(p * LANES, (p + 1) * LANES)
        q = qc_ref[0, :, cols]
        k = kc_ref[0, :, cols].astype(BF16)
        vaug = jnp.concatenate([vc_ref[0, :, cols].astype(BF16), jnp.ones((k.shape[0], LANES), BF16)], axis=1)
        outs = []
        for hh in range(2):
            qm = jnp.where(_head_lanes(hh), q, jnp.zeros_like(q)).astype(BF16)
            s = _dot_nt(qm, k)
            o = _dot(jnp.exp2(s - jnp.max(s, axis=-1, keepdims=True)).astype(BF16), vaug)
            outs.append(o[:, :LANES] * (1.0 / o[:, LANES:LANES + 1]))
        o = jnp.where(lo, outs[0], outs[1])
        yc_ref[0, :, cols] = (o * _silu(gc_ref[0, :, cols].astype(F32))).astype(yc_ref.dtype)


def _ctx_attn(z, blk, d_b, d_c, lam_vec, subln_g, layer, lam_init):
    B, L, _ = z.shape
    assert d_b == d_c
    spec = lambda name: pl.BlockSpec((1, L, d_b), lambda b, c=blk[name]: (b, 0, c))
    names = ("q_b", "k_b", "v_b", "g_b", "q_c", "k_c", "v_c", "g_c")
    return pl.pallas_call(
        functools.partial(_ctx_attn_kernel, layer=layer, lam_init=lam_init),
        grid=(B,),
        in_specs=[pl.BlockSpec(memory_space=pltpu.SMEM), *[spec(n) for n in names],
                  pl.BlockSpec((1, 2 * HEAD_DIM), lambda b: (0, 0))],
        out_specs=[pl.BlockSpec((1, L, d_b), lambda b: (b, 0, 0)), pl.BlockSpec((1, L, d_c), lambda b: (b, 0, 0))],
        out_shape=[jax.ShapeDtypeStruct((B, L, d_b), BF16), jax.ShapeDtypeStruct((B, L, d_c), BF16)],
        compiler_params=_cparams(("parallel",)),
        name="ctx_attn",
    )(lam_vec, *([z] * len(names)), subln_g)


def _na_kernel(q_ref, k_ref, v_ref, g_ref, kc_ref, vc_ref, bias_ref, o_ref, *, rows):
    lane = lax.broadcasted_iota(jnp.int32, (1, LANES), 1)
    n_loc = WIN_R * GRID_W
    P = kc_ref.shape[3]
    kc = _pair_lanes(kc_ref.at[0, 0])
    vc_aug = jnp.concatenate([_pair_lanes(vc_ref.at[0, 0]), jnp.ones((P, LANES), BF16)], axis=1)
    ones_loc = jnp.ones((n_loc, LANES), BF16)
    tq = NA_ROWS_PER_BLOCK * GRID_W
    lo = lane < HEAD_DIM

    def block(i, carry):
        q0 = pl.multiple_of(i * tq, tq)
        q = q_ref[0, pl.ds(q0, tq), :]
        zero = jnp.zeros_like(q)
        q_heads = (jnp.where(lo, q, zero), jnp.where(lo, zero, q))
        q2 = jnp.concatenate([qh[j * GRID_W:(j + 1) * GRID_W] for j in range(NA_ROWS_PER_BLOCK) for qh in q_heads],
                             axis=0)
        s_ctx = _dot_nt(q2, kc)
        m_ctx = jnp.max(s_ctx, axis=-1, keepdims=True)
        o_rows, p_ctx_rows = [], []
        for j in range(NA_ROWS_PER_BLOCK):
            r = i * NA_ROWS_PER_BLOCK + j
            rs = jnp.clip(r - WIN_R // 2, 0, rows - WIN_R)
            off = rs - r + WIN_R - 1
            k0 = pl.multiple_of(rs * GRID_W, GRID_W)
            kl = k_ref[0, pl.ds(k0, n_loc), :]
            vl_aug = jnp.concatenate([v_ref[0, pl.ds(k0, n_loc), :], ones_loc], axis=1)
            rsl = slice(2 * j * GRID_W, 2 * (j + 1) * GRID_W)
            s_loc = _dot_nt(q2[rsl], kl) + bias_ref[0, :, off].reshape(2 * GRID_W, n_loc)
            m = jnp.maximum(jnp.max(s_loc, axis=-1, keepdims=True), m_ctx[rsl])
            p_ctx_rows.append(jnp.exp2(s_ctx[rsl] - m).astype(BF16))
            o_rows.append(_dot(jnp.exp2(s_loc - m).astype(BF16), vl_aug))
        o = jnp.concatenate(o_rows, axis=0) + _dot(jnp.concatenate(p_ctx_rows, axis=0), vc_aug)
        o = o[:, :LANES] * (1.0 / o[:, LANES:LANES + 1])
        o = jnp.concatenate(
            [jnp.where(lo, o[2 * j * GRID_W:(2 * j + 1) * GRID_W], o[(2 * j + 1) * GRID_W:(2 * j + 2) * GRID_W])
             for j in range(NA_ROWS_PER_BLOCK)], axis=0)
        y = o * _silu(g_ref[0, pl.ds(q0, tq), :].astype(F32))
        o_ref[0, pl.ds(q0, tq), :] = y.astype(o_ref.dtype)
        return carry

    n_blocks = rows // NA_ROWS_PER_BLOCK
    lax.fori_loop(0, n_blocks, block, 0, unroll=math.gcd(n_blocks, NA_BLOCK_UNROLL))


def _na_attn(z, blk, d_c, nk, nv, bias_tab, layer):
    B, L, _ = z.shape
    rows = L // GRID_W
    assert rows >= WIN_R and L % GRID_W == 0 and rows % NA_ROWS_PER_BLOCK == 0
    P = nk.shape[3]
    n_pairs = d_c // LANES
    spec = lambda name: pl.BlockSpec((1, L, LANES), lambda b, p, c=blk[name] * n_pairs: (b, 0, c + p))
    cspec = pl.BlockSpec((1, 1, 2, P, HEAD_DIM), lambda b, p: (b, layer, p, 0, 0))
    return pl.pallas_call(
        functools.partial(_na_kernel, rows=rows),
        grid=(B, n_pairs),
        in_specs=[
            spec("q_c"), spec("k_c"), spec("v_c"), spec("g_c"), cspec, cspec,
            pl.BlockSpec((1, 2, WIN_R, GRID_W, WIN_R * GRID_W), lambda b, p: (layer, p, 0, 0, 0)),
        ],
        out_specs=pl.BlockSpec((1, L, LANES), lambda b, p: (b, 0, p)),
        out_shape=jax.ShapeDtypeStruct((B, L, d_c), BF16),
        compiler_params=_cparams(("parallel", "parallel")),
        name="na_attn_latent",
    )(z, z, z, z, nk, nv, bias_tab)


def _spatial_gate(u, v, g, ng_ref, nb_ref, ws_ref, bs_ref):
    n_groups = v.shape[1] // SG_GROUP_W
    vn = (_layernorm(v.astype(F32)) * ng_ref[...] + nb_ref[...]).astype(BF16)
    ys = []
    for n in range(v.shape[0] // CHUNK):
        rows = slice(n * CHUNK, (n + 1) * CHUNK)
        sv = jnp.concatenate(
            [_dot(ws_ref[gi], vn[rows, gi * SG_GROUP_W:(gi + 1) * SG_GROUP_W]) for gi in range(n_groups)],
            axis=1) + bs_ref[...]
        ys.append((u[rows].astype(F32) * sv * _silu(g[rows].astype(F32))).astype(BF16))
    return jnp.concatenate(ys, axis=0)


def _merge_kernel(x_ref, h_ref, u_ref, v_ref, g_ref, yb_ref, yc_ref, mod_ref, ng_ref, nb_ref, ws_ref, bs_ref,
                  wmg_ref, bmg_ref, wa_ref, wb_ref, wc_ref, wo_ref, lg_ref, lb_ref, o_ref, *, alpha, ts):
    D = x_ref.shape[2]
    for sub in range(x_ref.shape[1] // ts):
        rows = slice(sub * ts, (sub + 1) * ts)
        h = h_ref[0, rows, :]
        ya = _spatial_gate(u_ref[0, rows, :], v_ref[0, rows, :], g_ref[0, rows, :], ng_ref, nb_ref, ws_ref, bs_ref)
        m = None
        for i, (y, w_ref) in enumerate(((ya, wa_ref), (yb_ref[0, rows, :], wb_ref), (yc_ref[0, rows, :], wc_ref))):
            gate = _sigmoid(_dot(h, wmg_ref[:, i * D:(i + 1) * D]) + bmg_ref[:, i * D:(i + 1) * D])
            t = gate * _dot(y, w_ref[...])
            m = t if m is None else m + t
        out = _dot(m.astype(BF16), wo_ref[...])
        t = alpha * x_ref[0, rows, :] + mod_ref[0, 0, 2:3, :] * out
        o_ref[0, rows, :] = _layernorm(t) * lg_ref[...] + lb_ref[...]


def _merge(x, h, z, blk, yb, yc, mod, mod_row0, sg, wmg, bmg, wa, wb, wc, wo, lg, lb, alpha):
    B, L, D = x.shape
    d_a = wa.shape[0]
    tm = min(L, MERGE_TM)
    ts = tm // MERGE_SUBTILES
    assert ts % CHUNK == 0 and L % tm == 0
    tok = lambda w: pl.BlockSpec((1, tm, w), lambda b, i: (b, i, 0))
    zspec = lambda name: pl.BlockSpec((1, tm, d_a), lambda b, i, c=blk[name]: (b, i, c))
    const = lambda a: pl.BlockSpec(a.shape, lambda b, i: (0,) * a.ndim, pipeline_mode=pl.Buffered(1))
    weights = (*sg, wmg, bmg, wa, wb, wc, wo, lg, lb)
    return pl.pallas_call(
        functools.partial(_merge_kernel, alpha=alpha, ts=ts),
        grid=(B, L // tm),
        in_specs=[
            tok(D), tok(D), zspec("u_a"), zspec("v_a"), zspec("g_a"), tok(yb.shape[2]), tok(yc.shape[2]),
            pl.BlockSpec((1, 1, 3, D), lambda b, i: (mod_row0 + b, 0, 0, 0)),
            *[const(w) for w in weights],
        ],
        out_specs=tok(D),
        out_shape=jax.ShapeDtypeStruct((B, L, D), F32),
        compiler_params=_cparams(("parallel", "parallel")),
        name="merge_postnorm",
    )(x, h, z, z, z, yb, yc, mod, *weights)


def _rope_tables(n_tok):
    half = HEAD_DIM // 2
    quarter = HEAD_DIM // 4
    t = jnp.arange(n_tok)
    inv = 1.0 / (ROPE_THETA ** (jnp.arange(0, half, 2, dtype=F32) / half))
    ang = jnp.stack([(t // GRID_W).astype(F32)[:, None] * inv, (t % GRID_W).astype(F32)[:, None] * inv], axis=1)
    cos = jnp.cos(ang)[:, :, None, :]
    sin = jnp.sin(ang)[:, :, None, :]
    zero = jnp.zeros_like(sin)
    shape = (n_tok, 2, 2, quarter)
    c = jnp.broadcast_to(cos, shape).reshape(n_tok, HEAD_DIM)
    s_up = jnp.concatenate([-sin, zero], axis=2).reshape(n_tok, HEAD_DIM)
    s_dn = jnp.concatenate([zero, sin], axis=2).reshape(n_tok, HEAD_DIM)
    rep = LANES // HEAD_DIM
    return tuple(jnp.tile(a, (1, rep)) for a in (c, s_up, s_dn))


def _na_bias_kernel(rb_ref, o_ref):
    shape = (GRID_W, LANES)
    qcol = lax.broadcasted_iota(jnp.int32, shape, 0)
    lane = lax.broadcasted_iota(jnp.int32, shape, 1)
    kcol = lane & (GRID_W - 1)
    cstart = jnp.clip(qcol - WIN_W // 2, 0, GRID_W - WIN_W)
    in_window = (kcol >= cstart) & (kcol < cstart + WIN_W)

    def toeplitz(y, lane0):
        row = jnp.broadcast_to(rb_ref[0, y:y + 1, :], shape)
        return pltpu.roll(row, (lane0 - (WIN_W - 1)) % LANES, 1, stride=1, stride_axis=0)

    n_dy = 2 * WIN_R - 1
    t_lo = [toeplitz(y, 0) for y in range(n_dy)]
    t_hi = [toeplitz(y, GRID_W) for y in range(n_dy)]
    for off in range(WIN_R):
        for jp in range(WIN_R // 2):
            y = off + 2 * jp
            tile = jnp.where(lane < GRID_W, t_lo[y], t_hi[y + 1]) * LOG2E
            o_ref[0, 0, off, :, jp * LANES:(jp + 1) * LANES] = jnp.where(in_window, tile, NEG_INF)


def _na_bias_tables(rel_bias):
    depth, n_heads, n_dy, n_dx = rel_bias.shape
    assert n_dy == 2 * WIN_R - 1 and n_dx == 2 * WIN_W - 1 and WIN_R % 2 == 0
    rb = jnp.pad(rel_bias.astype(F32), ((0, 0), (0, 0), (0, 16 - n_dy), (0, LANES - n_dx)))
    return pl.pallas_call(
        _na_bias_kernel,
        grid=(depth, n_heads),
        in_specs=[pl.BlockSpec((1, 16, LANES), lambda l, h: (l * n_heads + h, 0, 0))],
        out_specs=pl.BlockSpec((1, 1, WIN_R, GRID_W, WIN_R * GRID_W), lambda l, h: (l, h, 0, 0, 0)),
        out_shape=jax.ShapeDtypeStruct((depth, n_heads, WIN_R, GRID_W, WIN_R * GRID_W), F32),
        compiler_params=_cparams(("parallel", "parallel")),
        name="na_bias_table",
    )(rb.reshape(depth * n_heads, 16, LANES))


def kernel(x_prompt, x_sample, c, cache_diff_k, cache_diff_v, cache_na_k, cache_na_v, c_ctx, w_ada, b_ada, w_in, sg_norm_g, sg_norm_b, w_spatial, b_spatial, lambda_q1, lambda_k1, lambda_q2, lambda_k2, diff_subln_g, na_rel_bias, w_br_a, w_br_b, w_br_c, w_mgate, b_mgate, w_out, ln_g, ln_b):
    depth, D, d_in = w_in.shape
    batch, seq, _ = x_prompt.shape
    dec_batch, dec_seq, _ = x_sample.shape
    d_a, d_b, d_c = w_br_a.shape[1], w_br_b.shape[1], w_br_c.shape[1]
    h_b = d_b // (2 * HEAD_DIM)
    h_c = d_c // HEAD_DIM
    tn = d_a
    assert d_a == d_b == d_c and d_in == 11 * tn and tn % (2 * LANES) == 0
    assert d_a // SG_GROUP_W == w_spatial.shape[1] and w_spatial.shape[2] == CHUNK
    names = ("u_a", "v_a", "g_a", "q_b", "k_b", "v_b", "g_b", "q_c", "k_c", "v_c", "g_c")
    blk = {n: i for i, n in enumerate(names)}
    alpha = (2 * depth) ** 0.25
    lam_inits = tuple(0.8 - 0.6 * math.exp(-0.3 * l) for l in range(depth))

    mod_rows = -(-(1 + dec_batch) // 8) * 8
    cvec = jnp.zeros((mod_rows, D), F32).at[0].set(c_ctx).at[1:1 + dec_batch].set(c)
    mod = _ada_mod(cvec, w_ada, b_ada).reshape(depth * mod_rows, 1, 3, D)
    lam_vec = _diff_lambdas(lambda_q1, lambda_k1, lambda_q2, lambda_k2, lam_inits)

    w_in_b = w_in.astype(BF16)
    w_s_b = w_spatial.astype(BF16)
    b_s_full = jnp.repeat(jnp.swapaxes(b_spatial, 1, 2), SG_GROUP_W, axis=2)
    wmg_b, wa_b, wb_b, wc_b, wo_b = (w.astype(BF16) for w in (w_mgate, w_br_a, w_br_b, w_br_c, w_out))
    rope_tabs = _rope_tables(dec_seq)
    bias_tab = _na_bias_tables(na_rel_bias)

    xp = x_prompt.reshape(1, batch * seq, D)
    xs = x_sample
    new_dk, new_dv, new_nk, new_nv = [], [], [], []
    for l in range(depth):
        row2 = lambda a: a[l].reshape(1, -1)
        merge_w = (wmg_b[l], row2(b_mgate), wa_b[l], wb_b[l], wc_b[l], wo_b[l], row2(ln_g), row2(ln_b))
        sg = (row2(sg_norm_g), row2(sg_norm_b), w_s_b[l], b_s_full[l])

        h, z = _inproj(xp, mod, l * mod_rows, w_in_b[l], blk, tn, F32)
        zc = z.reshape(batch, seq, d_in)
        yb, yc = _ctx_attn(zc, blk, d_b, d_c, lam_vec, row2(diff_subln_g), l, lam_inits[l])
        flat = lambda a: a.reshape(1, batch * seq, a.shape[-1])
        xp = _merge(xp, h, z, blk, flat(yb), flat(yc), mod, l * mod_rows, sg, *merge_w, alpha)
        col = lambda n: zc[:, :, blk[n] * tn:(blk[n] + 1) * tn]
        new_dk.append(col("k_b").reshape(batch, seq, 2, h_b, HEAD_DIM).transpose(0, 2, 3, 1, 4))
        new_dv.append(col("v_b").reshape(batch, seq, h_b, 2 * HEAD_DIM).transpose(0, 2, 1, 3))
        new_nk.append(col("k_c").reshape(batch, seq, h_c, HEAD_DIM).transpose(0, 2, 1, 3))
        new_nv.append(col("v_c").reshape(batch, seq, h_c, HEAD_DIM).transpose(0, 2, 1, 3))

        h, z, vt = _inproj(xs, mod, l * mod_rows + 1, w_in_b[l], blk, tn, BF16, rope_tabs)
        yb = _diff_attn(z, vt, blk, d_b, lam_vec, row2(diff_subln_g), l, lam_inits[l],
                        cache=(cache_diff_k, cache_diff_v))
        yc = _na_attn(z, blk, d_c, cache_na_k, cache_na_v, bias_tab, l)
        xs = _merge(xs, h, z, blk, yb, yc, mod, l * mod_rows + 1, sg, *merge_w, alpha)

    return (xp.reshape(batch, seq, D), xs,
            jnp.stack(new_dk, axis=1), jnp.stack(new_dv, axis=1),
            jnp.stack(new_nk, axis=1), jnp.stack(new_nv, axis=1))
```

```python
import functools
import math

import jax
import jax.numpy as jnp
from jax import lax
from jax.experimental import pallas as pl
from jax.experimental.pallas import tpu as pltpu

GRID_W = 64
CHUNK = 128
HEAD_DIM = 64
SG_GROUP_W = 128
WIN_R = 8
WIN_W = 16
DA_TQ = 256
DA_TK = 256
DA_SUBTILES = 2
DA_ONES_ROWS = 16
INPROJ_SUBTILES = 2
MERGE_TM = 512
MERGE_SUBTILES = 2
NA_ROWS_PER_BLOCK = 4
NA_BLOCK_UNROLL = 8
ROPE_THETA = 10000.0
EPS = 1e-6
NEG_INF = -1e30
LANES = 128
LOG2E = math.log2(math.e)
Q_SCALE = HEAD_DIM ** -0.5 * LOG2E

BF16 = jnp.bfloat16
F32 = jnp.float32

VMEM_LIMIT = 56 * 1024 * 1024


def _cparams(sem):
    return pltpu.CompilerParams(dimension_semantics=sem, vmem_limit_bytes=VMEM_LIMIT)


def _sigmoid(x):
    return 1.0 / (1.0 + jnp.exp(-x))


def _silu(x):
    return x * _sigmoid(x)


def _dot(a, b):
    return jnp.dot(a, b, preferred_element_type=F32)


def _dot_nt(a, b):
    return lax.dot_general(a, b, (((1,), (1,)), ((), ())), preferred_element_type=F32)


def _layernorm(x):
    mu = jnp.mean(x, axis=-1, keepdims=True)
    xc = x - mu
    var = jnp.mean(xc * xc, axis=-1, keepdims=True)
    return xc * lax.rsqrt(var + EPS)


def _ada_kernel(c_ref, w_ref, b_ref, o_ref):
    s = _silu(c_ref[...]).astype(BF16)
    o_ref[0] = _dot(s, w_ref[0].astype(BF16)) + b_ref[0]


def _ada_mod(cvec, w_ada, b_ada):
    depth, d, d3 = w_ada.shape
    rows = cvec.shape[0]
    tn = 1024
    return pl.pallas_call(
        _ada_kernel,
        grid=(depth, d3 // tn),
        in_specs=[
            pl.BlockSpec((rows, d), lambda l, j: (0, 0)),
            pl.BlockSpec((1, d, tn), lambda l, j: (l, 0, j)),
            pl.BlockSpec((1, 1, tn), lambda l, j: (l, 0, j)),
        ],
        out_specs=pl.BlockSpec((1, rows, tn), lambda l, j: (l, 0, j)),
        out_shape=jax.ShapeDtypeStruct((depth, rows, d3), F32),
        compiler_params=_cparams(("parallel", "parallel")),
        name="ada_mod",
    )(cvec, w_ada, b_ada.reshape(depth, 1, d3))


def _lambda_kernel(lam_ref, o_ref, *, lam_inits):
    p = lam_ref[...]
    t1 = jnp.sum(p[:, 0, :] * p[:, 1, :], axis=-1, keepdims=True)
    t2 = jnp.sum(p[:, 2, :] * p[:, 3, :], axis=-1, keepdims=True)
    layer = lax.broadcasted_iota(jnp.int32, t1.shape, 0)
    init = jnp.zeros_like(t1)
    for l, v in enumerate(lam_inits):
        init = jnp.where(layer == l, v, init)
    o_ref[...] = jnp.broadcast_to(jnp.exp(t1) - jnp.exp(t2) + init, o_ref.shape)


def _diff_lambdas(lq1, lk1, lq2, lk2, lam_inits):
    depth = lq1.shape[0]
    params = jnp.stack([lq1, lk1, lq2, lk2], axis=1).astype(F32)
    out = pl.pallas_call(
        functools.partial(_lambda_kernel, lam_inits=lam_inits),
        out_shape=jax.ShapeDtypeStruct((depth, LANES), F32),
        name="diff_lambda",
    )(params)
    return out[:, 0]


def _rope(a, c, s_up, s_dn):
    outs = []
    for i in range(a.shape[1] // LANES):
        g = a[:, i * LANES:(i + 1) * LANES]
        outs.append(g * c + pltpu.roll(g, LANES - 16, 1) * s_up + pltpu.roll(g, 16, 1) * s_dn)
    return jnp.concatenate(outs, axis=1)


def _store_heads(out_ref, lead, rows, acc):
    width = out_ref.shape[-1]
    for hd in range(out_ref.shape[-3]):
        out_ref[(*lead, hd, rows, slice(None))] = acc[:, hd * width:(hd + 1) * width]


def _inproj_kernel(*refs, rope, blk, tn, ts):
    if rope:
        x_ref, mod_ref, w_ref, c_ref, su_ref, sd_ref, h_ref, z_ref, vt_ref = refs
    else:
        x_ref, mod_ref, w_ref, _, _, _, _, h_ref, z_ref, dk_ref, dv_ref, nk_ref, nv_ref = refs
    for sub in range(x_ref.shape[1] // ts):
        rows = slice(sub * ts, (sub + 1) * ts)
        xn = _layernorm(x_ref[0, rows, :])
        hb = (xn * (1.0 + mod_ref[0, 0, 1:2, :]) + mod_ref[0, 0, 0:1, :]).astype(BF16)
        h_ref[0, rows, :] = hb
        for j in range(w_ref.shape[1] // tn):
            cols = slice(j * tn, (j + 1) * tn)
            acc = _dot(hb, w_ref[:, cols])
            if rope and j in (blk["q_b"], blk["k_b"]):
                acc = _rope(acc, c_ref[rows, :], su_ref[rows, :], sd_ref[rows, :])
            if j in (blk["q_b"], blk["q_c"]):
                acc = acc * Q_SCALE
            z_ref[0, rows, cols] = acc.astype(z_ref.dtype)
            if rope and j == blk["v_b"]:
                vt_ref[0, :, rows] = acc.T.astype(vt_ref.dtype)
            if not rope:
                if j == blk["k_b"]:
                    for m in range(dk_ref.shape[2]):
                        half = dk_ref.shape[3] * dk_ref.shape[5]
                        _store_heads(dk_ref, (0, 0, m), rows, acc[:, m * half:(m + 1) * half])
                elif j == blk["v_b"]:
                    _store_heads(dv_ref, (0, 0), rows, acc)
                elif j == blk["k_c"]:
                    _store_heads(nk_ref, (0, 0), rows, acc)
                elif j == blk["v_c"]:
                    _store_heads(nv_ref, (0, 0), rows, acc)


def _inproj(x, mod, mod_row0, w, blk, tn, rope_tabs=None, new_cache=None, layer=None):
    B, L, D = x.shape
    d_in = w.shape[1]
    rope = rope_tabs is not None
    tm = min(L, 512) if rope else new_cache[0].shape[-2]
    in_specs = [
        pl.BlockSpec((1, tm, D), lambda b, i: (b, i, 0)),
        pl.BlockSpec((1, 1, 3, D), lambda b, i: (mod_row0 + b, 0, 0, 0)),
        pl.BlockSpec((D, d_in), lambda b, i: (0, 0), pipeline_mode=pl.Buffered(1)),
    ]
    args = [x, mod, w]
    out_specs = [
        pl.BlockSpec((1, tm, D), lambda b, i: (b, i, 0)),
        pl.BlockSpec((1, tm, d_in), lambda b, i: (b, i, 0)),
    ]
    out_shape = [
        jax.ShapeDtypeStruct((B, L, D), BF16),
        jax.ShapeDtypeStruct((B, L, d_in), BF16 if rope else F32),
    ]
    aliases = {}
    if rope:
        in_specs += [pl.BlockSpec((tm, LANES), lambda b, i: (i, 0))] * 3
        args += list(rope_tabs)
        out_specs.append(pl.BlockSpec((1, tn, tm), lambda b, i: (b, 0, i)))
        out_shape.append(jax.ShapeDtypeStruct((B, tn, L), BF16))
    else:
        assert B == 1 and L == new_cache[0].shape[0] * tm and tm % INPROJ_SUBTILES == 0
        for a in new_cache:
            aliases[len(args)] = len(out_shape)
            in_specs.append(pl.BlockSpec(memory_space=pl.ANY))
            args.append(a)
            tail = a.shape[2:]
            out_specs.append(pl.BlockSpec((1, 1, *tail), lambda b, i, n=len(tail): (i, layer) + (0,) * n))
            out_shape.append(jax.ShapeDtypeStruct(a.shape, a.dtype))
    return pl.pallas_call(
        functools.partial(_inproj_kernel, rope=rope, blk=blk, tn=tn, ts=tm // INPROJ_SUBTILES),
        grid=(B, L // tm),
        in_specs=in_specs,
        out_specs=out_specs,
        out_shape=out_shape,
        input_output_aliases=aliases,
        compiler_params=_cparams(("parallel", "parallel")),
        name="inproj_rope" if rope else "inproj",
    )(*args)


def _head_lanes(hh):
    lane = lax.broadcasted_iota(jnp.int32, (1, LANES), 1)
    return (lane < HEAD_DIM) if hh == 0 else (lane >= HEAD_DIM)


def _diff_head(q1, q2, chunks, hh, lam, lam_init, sg, g):
    tq = q1.shape[0]
    dv = 2 * HEAD_DIM
    qm = [jnp.where(_head_lanes(hh), q, jnp.zeros_like(q)).astype(BF16) for q in (q1, q2)]
    state = [(jnp.full((tq, 1), -jnp.inf, F32), jnp.zeros((tq, 2 * dv), F32)) for _ in range(2)]
    for k1c, k2c, vc in chunks:
        ones = jnp.ones((vc.shape[0], LANES), BF16)
        vaug = jnp.concatenate([vc[:, hh * dv:(hh + 1) * dv].astype(BF16), ones], axis=1)
        for m, kc in enumerate((k1c, k2c)):
            mx, acc = state[m]
            s = _dot_nt(qm[m], kc[...].astype(BF16))
            mx_new = jnp.maximum(mx, jnp.max(s, axis=-1, keepdims=True))
            p = jnp.exp2(s - mx_new).astype(BF16)
            state[m] = (mx_new, jnp.exp2(mx - mx_new) * acc + _dot(p, vaug))
    (_, a1), (_, a2) = state
    o = a1[:, :dv] * (1.0 / a1[:, dv:dv + 1]) - a2[:, :dv] * (lam / a2[:, dv:dv + 1])
    of = o * lax.rsqrt(jnp.mean(o * o, axis=-1, keepdims=True) + EPS)
    return of * sg * (1.0 - lam_init) * _silu(g.astype(F32))


def _pair_lanes(ref):
    return jnp.concatenate([ref[0], ref[1]], axis=-1).astype(BF16)


def _diff_attn_kernel(lam_ref, q1_ref, q2_ref, k1_ref, k2_ref, vt_ref, g_ref, kc1_ref, kc2_ref, vc_ref, sg_ref,
                      o_ref, *, layer, lam_init, tq, tk, n_self, n_ctx):
    lam = lam_ref[layer]
    dv = 2 * HEAD_DIM
    ones = jnp.ones((DA_ONES_ROWS, tk), BF16)
    kc = [_pair_lanes(r.at[0, 0, 0]) for r in (kc1_ref, kc2_ref)]
    vct = jnp.concatenate([vc_ref[0, 0, hd].T for hd in range(2)], axis=0).astype(BF16)
    chunks = [(k1_ref.at[0, c * tk:(c + 1) * tk, :], k2_ref.at[0, c * tk:(c + 1) * tk, :],
               vt_ref.at[0, :, c * tk:(c + 1) * tk]) for c in range(n_self)]
    chunks += [(kc[0][c * tk:(c + 1) * tk], kc[1][c * tk:(c + 1) * tk], vct[:, c * tk:(c + 1) * tk])
               for c in range(n_ctx)]
    units = [(sub, hh, m) for sub in range(q1_ref.shape[1] // tq) for hh in range(2) for m in range(2)]
    qm, state = {}, {}
    for u in units:
        sub, hh, m = u
        q = (q1_ref, q2_ref)[m][0, sub * tq:(sub + 1) * tq, :]
        qm[u] = jnp.where(_head_lanes(hh), q, jnp.zeros_like(q)).astype(BF16)
        state[u] = (jnp.full((1, tq), -jnp.inf, F32), jnp.zeros((dv + DA_ONES_ROWS, tq), F32))

    def scores(u, chunk):
        return _dot_nt(chunk[u[2]][...], qm[u])

    def update(u, chunk, st):
        hh = u[1]
        vt_aug = jnp.concatenate([chunk[2][hh * dv:(hh + 1) * dv, :], ones], axis=0)
        mx, acc = state[u]
        mx_new = jnp.maximum(mx, jnp.max(st, axis=0, keepdims=True))
        pt = jnp.exp2(st - mx_new).astype(BF16)
        state[u] = (mx_new, jnp.exp2(mx - mx_new) * acc + _dot(vt_aug, pt))

    cur = {u: scores(u, chunks[0]) for u in units}
    for c, chunk in enumerate(chunks):
        nxt = {}
        for u in units:
            if c + 1 < len(chunks):
                nxt[u] = scores(u, chunks[c + 1])
            update(u, chunk, cur[u])
        cur = nxt

    for sub in range(q1_ref.shape[1] // tq):
        rows = slice(sub * tq, (sub + 1) * tq)
        for hh in range(2):
            cols = slice(hh * dv, (hh + 1) * dv)
            a1, a2 = state[sub, hh, 0][1], state[sub, hh, 1][1]
            ot = a1[:dv] * (1.0 / a1[dv:dv + 1]) - a2[:dv] * (lam / a2[dv:dv + 1])
            oft = ot * lax.rsqrt(jnp.mean(ot * ot, axis=0, keepdims=True) + EPS)
            y = oft.T * sg_ref[...] * (1.0 - lam_init) * _silu(g_ref[0, rows, cols].astype(F32))
            o_ref[0, rows, cols] = y.astype(o_ref.dtype)


def _diff_attn(z, vt, blk, d_b, lam_vec, subln_g, layer, lam_init, cache):
    B, L, _ = z.shape
    tq = min(L, DA_TQ)
    bq = min(L, DA_SUBTILES * tq)
    tk = min(L, DA_TK)
    n_pairs = d_b // (4 * HEAD_DIM)
    col128 = lambda name: blk[name] * (d_b // LANES)
    col256 = lambda name: blk[name] * (d_b // (2 * LANES))
    qspec = lambda m: pl.BlockSpec((1, bq, LANES), lambda b, p, i: (b, i, col128("q_b") + m * n_pairs + p))
    kspec = lambda m: pl.BlockSpec((1, L, LANES), lambda b, p, i: (b, 0, col128("k_b") + m * n_pairs + p))
    ck, cv = cache
    P = ck.shape[4]
    assert P % tk == 0 and L % tk == 0 and L % bq == 0
    kcspec = lambda m: pl.BlockSpec((1, 1, 1, 2, P, HEAD_DIM), lambda b, p, i: (b, layer, m, p, 0, 0))
    in_specs = [
        pl.BlockSpec(memory_space=pltpu.SMEM),
        qspec(0), qspec(1), kspec(0), kspec(1),
        pl.BlockSpec((1, 2 * LANES, L), lambda b, p, i: (b, p, 0)),
        pl.BlockSpec((1, bq, 2 * LANES), lambda b, p, i: (b, i, col256("g_b") + p)),
        kcspec(0), kcspec(1),
        pl.BlockSpec((1, 1, 2, P, 2 * HEAD_DIM), lambda b, p, i: (b, layer, p, 0, 0)),
        pl.BlockSpec((1, 2 * HEAD_DIM), lambda b, p, i: (0, 0)),
    ]
    return pl.pallas_call(
        functools.partial(_diff_attn_kernel, layer=layer, lam_init=lam_init, tq=tq, tk=tk, n_self=L // tk,
                          n_ctx=P // tk),
        grid=(B, n_pairs, L // bq),
        in_specs=in_specs,
        out_specs=pl.BlockSpec((1, bq, 2 * LANES), lambda b, p, i: (b, i, p)),
        out_shape=jax.ShapeDtypeStruct((B, L, d_b), BF16),
        compiler_params=_cparams(("parallel", "parallel", "parallel")),
        name="diff_attn_latent",
    )(lam_vec, z, z, z, z, vt, z, ck, ck, cv, subln_g)


def _ctx_attn_kernel(lam_ref, qb_ref, kb_ref, vb_ref, gb_ref, qc_ref, kc_ref, vc_ref, gc_ref, sg_ref,
                     yb_ref, yc_ref, *, layer, lam_init):
    dv = 2 * HEAD_DIM
    n_pairs_b = qb_ref.shape[2] // (2 * LANES)
    for p in range(n_pairs_b):
        blk1 = slice(p * LANES, (p + 1) * LANES)
        blk2 = slice((n_pairs_b + p) * LANES, (n_pairs_b + p + 1) * LANES)
        vcols = slice(p * 2 * dv, (p + 1) * 2 * dv)
        chunks = [(kb_ref.at[0, :, blk1], kb_ref.at[0, :, blk2], vb_ref.at[0, :, vcols])]
        for hh in range(2):
            cols = slice((2 * p + hh) * dv, (2 * p + hh + 1) * dv)
            y = _diff_head(qb_ref[0, :, blk1], qb_ref[0, :, blk2], chunks, hh, lam_ref[layer], lam_init,
                           sg_ref[...], gb_ref[0, :, cols])
            yb_ref[0, :, cols] = y.astype(yb_ref.dtype)

    lo = _head_lanes(0)
    for p in range(qc_ref.shape[2] // LANES):
        cols = slice(p * LANES, (p + 1) * LANES)
        q = qc_ref[0, :, cols]
        k = kc_ref[0, :, cols].astype(BF16)
        vaug = jnp.concatenate([vc_ref[0, :, cols].astype(BF16), jnp.ones((k.shape[0], LANES), BF16)], axis=1)
        outs = []
        for hh in range(2):
            qm = jnp.where(_head_lanes(hh), q, jnp.zeros_like(q)).astype(BF16)
            s = _dot_nt(qm, k)
            o = _dot(jnp.exp2(s - jnp.max(s, axis=-1, keepdims=True)).astype(BF16), vaug)
            outs.append(o[:, :LANES] * (1.0 / o[:, LANES:LANES + 1]))
        o = jnp.where(lo, outs[0], outs[1])
        yc_ref[0, :, cols] = (o * _silu(gc_ref[0, :, cols].astype(F32))).astype(yc_ref.dtype)


def _ctx_attn(z, blk, d_b, d_c, lam_vec, subln_g, layer, lam_init):
    B, L, _ = z.shape
    assert d_b == d_c
    spec = lambda name: pl.BlockSpec((1, L, d_b), lambda b, c=blk[name]: (b, 0, c))
    names = ("q_b", "k_b", "v_b", "g_b", "q_c", "k_c", "v_c", "g_c")
    return pl.pallas_call(
        functools.partial(_ctx_attn_kernel, layer=layer, lam_init=lam_init),
        grid=(B,),
        in_specs=[pl.BlockSpec(memory_space=pltpu.SMEM), *[spec(n) for n in names],
                  pl.BlockSpec((1, 2 * HEAD_DIM), lambda b: (0, 0))],
        out_specs=[pl.BlockSpec((1, L, d_b), lambda b: (b, 0, 0)), pl.BlockSpec((1, L, d_c), lambda b: (b, 0, 0))],
        out_shape=[jax.ShapeDtypeStruct((B, L, d_b), BF16), jax.ShapeDtypeStruct((B, L, d_c), BF16)],
        compiler_params=_cparams(("parallel",)),
        name="ctx_attn",
    )(lam_vec, *([z] * len(names)), subln_g)


def _na_kernel(q_ref, k_ref, v_ref, g_ref, kc_ref, vc_ref, bias_ref, o_ref, *, rows):
    lane = lax.broadcasted_iota(jnp.int32, (1, LANES), 1)
    n_loc = WIN_R * GRID_W
    P = kc_ref.shape[3]
    kc = _pair_lanes(kc_ref.at[0, 0])
    vc_aug = jnp.concatenate([_pair_lanes(vc_ref.at[0, 0]), jnp.ones((P, LANES), BF16)], axis=1)
    ones_loc = jnp.ones((n_loc, LANES), BF16)
    tq = NA_ROWS_PER_BLOCK * GRID_W
    lo = lane < HEAD_DIM

    def block(i, carry):
        q0 = pl.multiple_of(i * tq, tq)
        q = q_ref[0, pl.ds(q0, tq), :]
        zero = jnp.zeros_like(q)
        q_heads = (jnp.where(lo, q, zero), jnp.where(lo, zero, q))
        q2 = jnp.concatenate([qh[j * GRID_W:(j + 1) * GRID_W] for j in range(NA_ROWS_PER_BLOCK) for qh in q_heads],
                             axis=0)
        s_ctx = _dot_nt(q2, kc)
        m_ctx = jnp.max(s_ctx, axis=-1, keepdims=True)
        o_rows, p_ctx_rows = [], []
        for j in range(NA_ROWS_PER_BLOCK):
            r = i * NA_ROWS_PER_BLOCK + j
            rs = jnp.clip(r - WIN_R // 2, 0, rows - WIN_R)
            off = rs - r + WIN_R - 1
            k0 = pl.multiple_of(rs * GRID_W, GRID_W)
            kl = k_ref[0, pl.ds(k0, n_loc), :]
            vl_aug = jnp.concatenate([v_ref[0, pl.ds(k0, n_loc), :], ones_loc], axis=1)
            rsl = slice(2 * j * GRID_W, 2 * (j + 1) * GRID_W)
            s_loc = _dot_nt(q2[rsl], kl) + bias_ref[0, :, off].reshape(2 * GRID_W, n_loc)
            m = jnp.maximum(jnp.max(s_loc, axis=-1, keepdims=True), m_ctx[rsl])
            p_ctx_rows.append(jnp.exp2(s_ctx[rsl] - m).astype(BF16))
            o_rows.append(_dot(jnp.exp2(s_loc - m).astype(BF16), vl_aug))
        o = jnp.concatenate(o_rows, axis=0) + _dot(jnp.concatenate(p_ctx_rows, axis=0), vc_aug)
        o = o[:, :LANES] * (1.0 / o[:, LANES:LANES + 1])
        o = jnp.concatenate(
            [jnp.where(lo, o[2 * j * GRID_W:(2 * j + 1) * GRID_W], o[(2 * j + 1) * GRID_W:(2 * j + 2) * GRID_W])
             for j in range(NA_ROWS_PER_BLOCK)], axis=0)
        y = o * _silu(g_ref[0, pl.ds(q0, tq), :].astype(F32))
        o_ref[0, pl.ds(q0, tq), :] = y.astype(o_ref.dtype)
        return carry

    n_blocks = rows // NA_ROWS_PER_BLOCK
    lax.fori_loop(0, n_blocks, block, 0, unroll=math.gcd(n_blocks, NA_BLOCK_UNROLL))


def _na_attn(z, blk, d_c, nk, nv, bias_tab, layer):
    B, L, _ = z.shape
    rows = L // GRID_W
    assert rows >= WIN_R and L % GRID_W == 0 and rows % NA_ROWS_PER_BLOCK == 0
    P = nk.shape[3]
    n_pairs = d_c // LANES
    spec = lambda name: pl.BlockSpec((1, L, LANES), lambda b, p, c=blk[name] * n_pairs: (b, 0, c + p))
    cspec = pl.BlockSpec((1, 1, 2, P, HEAD_DIM), lambda b, p: (b, layer, p, 0, 0))
    return pl.pallas_call(
        functools.partial(_na_kernel, rows=rows),
        grid=(B, n_pairs),
        in_specs=[
            spec("q_c"), spec("k_c"), spec("v_c"), spec("g_c"), cspec, cspec,
            pl.BlockSpec((1, 2, WIN_R, GRID_W, WIN_R * GRID_W), lambda b, p: (layer, p, 0, 0, 0)),
        ],
        out_specs=pl.BlockSpec((1, L, LANES), lambda b, p: (b, 0, p)),
        out_shape=jax.ShapeDtypeStruct((B, L, d_c), BF16),
        compiler_params=_cparams(("parallel", "parallel")),
        name="na_attn_latent",
    )(z, z, z, z, nk, nv, bias_tab)


def _spatial_gate(u, v, g, ng_ref, nb_ref, ws_ref, bs_ref):
    n_groups = v.shape[1] // SG_GROUP_W
    vn = (_layernorm(v.astype(F32)) * ng_ref[...] + nb_ref[...]).astype(BF16)
    ys = []
    for n in range(v.shape[0] // CHUNK):
        rows = slice(n * CHUNK, (n + 1) * CHUNK)
        sv = jnp.concatenate(
            [_dot(ws_ref[gi], vn[rows, gi * SG_GROUP_W:(gi + 1) * SG_GROUP_W]) for gi in range(n_groups)],
            axis=1) + bs_ref[...]
        ys.append((u[rows].astype(F32) * sv * _silu(g[rows].astype(F32))).astype(BF16))
    return jnp.concatenate(ys, axis=0)


def _merge_kernel(x_ref, h_ref, u_ref, v_ref, g_ref, yb_ref, yc_ref, mod_ref, ng_ref, nb_ref, ws_ref, bs_ref,
                  wmg_ref, bmg_ref, wa_ref, wb_ref, wc_ref, wo_ref, lg_ref, lb_ref, o_ref, *, alpha, ts):
    D = x_ref.shape[2]
    for sub in range(x_ref.shape[1] // ts):
        rows = slice(sub * ts, (sub + 1) * ts)
        h = h_ref[0, rows, :]
        ya = _spatial_gate(u_ref[0, rows, :], v_ref[0, rows, :], g_ref[0, rows, :], ng_ref, nb_ref, ws_ref, bs_ref)
        m = None
        for i, (y, w_ref) in enumerate(((ya, wa_ref), (yb_ref[0, rows, :], wb_ref), (yc_ref[0, rows, :], wc_ref))):
            gate = _sigmoid(_dot(h, wmg_ref[:, i * D:(i + 1) * D]) + bmg_ref[:, i * D:(i + 1) * D])
            t = gate * _dot(y, w_ref[...])
            m = t if m is None else m + t
        out = _dot(m.astype(BF16), wo_ref[...])
        t = alpha * x_ref[0, rows, :] + mod_ref[0, 0, 2:3, :] * out
        o_ref[0, rows, :] = _layernorm(t) * lg_ref[...] + lb_ref[...]


def _merge(x, h, z, blk, yb, yc, mod, mod_row0, sg, wmg, bmg, wa, wb, wc, wo, lg, lb, alpha):
    B, L, D = x.shape
    d_a = wa.shape[0]
    tm = min(L, MERGE_TM)
    ts = tm // MERGE_SUBTILES
    assert ts % CHUNK == 0 and L % tm == 0
    tok = lambda w: pl.BlockSpec((1, tm, w), lambda b, i: (b, i, 0))
    zspec = lambda name: pl.BlockSpec((1, tm, d_a), lambda b, i, c=blk[name]: (b, i, c))
    const = lambda a: pl.BlockSpec(a.shape, lambda b, i: (0,) * a.ndim, pipeline_mode=pl.Buffered(1))
    weights = (*sg, wmg, bmg, wa, wb, wc, wo, lg, lb)
    return pl.pallas_call(
        functools.partial(_merge_kernel, alpha=alpha, ts=ts),
        grid=(B, L // tm),
        in_specs=[
            tok(D), tok(D), zspec("u_a"), zspec("v_a"), zspec("g_a"), tok(yb.shape[2]), tok(yc.shape[2]),
            pl.BlockSpec((1, 1, 3, D), lambda b, i: (mod_row0 + b, 0, 0, 0)),
            *[const(w) for w in weights],
        ],
        out_specs=tok(D),
        out_shape=jax.ShapeDtypeStruct((B, L, D), F32),
        compiler_params=_cparams(("parallel", "parallel")),
        name="merge_postnorm",
    )(x, h, z, z, z, yb, yc, mod, *weights)


def _rope_tables(n_tok):
    half = HEAD_DIM // 2
    quarter = HEAD_DIM // 4
    t = jnp.arange(n_tok)
    inv = 1.0 / (ROPE_THETA ** (jnp.arange(0, half, 2, dtype=F32) / half))
    ang = jnp.stack([(t // GRID_W).astype(F32)[:, None] * inv, (t % GRID_W).astype(F32)[:, None] * inv], axis=1)
    cos = jnp.cos(ang)[:, :, None, :]
    sin = jnp.sin(ang)[:, :, None, :]
    zero = jnp.zeros_like(sin)
    shape = (n_tok, 2, 2, quarter)
    c = jnp.broadcast_to(cos, shape).reshape(n_tok, HEAD_DIM)
    s_up = jnp.concatenate([-sin, zero], axis=2).reshape(n_tok, HEAD_DIM)
    s_dn = jnp.concatenate([zero, sin], axis=2).reshape(n_tok, HEAD_DIM)
    rep = LANES // HEAD_DIM
    return tuple(jnp.tile(a, (1, rep)) for a in (c, s_up, s_dn))


def _na_bias_kernel(rb_ref, o_ref):
    shape = (GRID_W, LANES)
    qcol = lax.broadcasted_iota(jnp.int32, shape, 0)
    lane = lax.broadcasted_iota(jnp.int32, shape, 1)
    kcol = lane & (GRID_W - 1)
    cstart = jnp.clip(qcol - WIN_W // 2, 0, GRID_W - WIN_W)
    in_window = (kcol >= cstart) & (kcol < cstart + WIN_W)

    def toeplitz(y, lane0):
        row = jnp.broadcast_to(rb_ref[0, y:y + 1, :], shape)
        return pltpu.roll(row, (lane0 - (WIN_W - 1)) % LANES, 1, stride=1, stride_axis=0)

    n_dy = 2 * WIN_R - 1
    t_lo = [toeplitz(y, 0) for y in range(n_dy)]
    t_hi = [toeplitz(y, GRID_W) for y in range(n_dy)]
    for off in range(WIN_R):
        for jp in range(WIN_R // 2):
            y = off + 2 * jp
            tile = jnp.where(lane < GRID_W, t_lo[y], t_hi[y + 1]) * LOG2E
            o_ref[0, 0, off, :, jp * LANES:(jp + 1) * LANES] = jnp.where(in_window, tile, NEG_INF)


def _na_bias_tables(rel_bias):
    depth, n_heads, n_dy, n_dx = rel_bias.shape
    assert n_dy == 2 * WIN_R - 1 and n_dx == 2 * WIN_W - 1 and WIN_R % 2 == 0
    rb = jnp.pad(rel_bias.astype(F32), ((0, 0), (0, 0), (0, 16 - n_dy), (0, LANES - n_dx)))
    return pl.pallas_call(
        _na_bias_kernel,
        grid=(depth, n_heads),
        in_specs=[pl.BlockSpec((1, 16, LANES), lambda l, h: (l * n_heads + h, 0, 0))],
        out_specs=pl.BlockSpec((1, 1, WIN_R, GRID_W, WIN_R * GRID_W), lambda l, h: (l, h, 0, 0, 0)),
        out_shape=jax.ShapeDtypeStruct((depth, n_heads, WIN_R, GRID_W, WIN_R * GRID_W), F32),
        compiler_params=_cparams(("parallel", "parallel")),
        name="na_bias_table",
    )(rb.reshape(depth * n_heads, 16, LANES))


def kernel(x_prompt, x_sample, c, cache_diff_k, cache_diff_v, cache_na_k, cache_na_v, c_ctx, w_ada, b_ada, w_in, sg_norm_g, sg_norm_b, w_spatial, b_spatial, lambda_q1, lambda_k1, lambda_q2, lambda_k2, diff_subln_g, na_rel_bias, w_br_a, w_br_b, w_br_c, w_mgate, b_mgate, w_out, ln_g, ln_b):
    depth, D, d_in = w_in.shape
    batch, seq, _ = x_prompt.shape
    dec_batch, dec_seq, _ = x_sample.shape
    d_a, d_b, d_c = w_br_a.shape[1], w_br_b.shape[1], w_br_c.shape[1]
    h_b = d_b // (2 * HEAD_DIM)
    h_c = d_c // HEAD_DIM
    tn = d_a
    assert d_a == d_b == d_c and d_in == 11 * tn and tn % (2 * LANES) == 0
    assert d_a // SG_GROUP_W == w_spatial.shape[1] and w_spatial.shape[2] == CHUNK
    names = ("u_a", "v_a", "g_a", "q_b", "k_b", "v_b", "g_b", "q_c", "k_c", "v_c", "g_c")
    blk = {n: i for i, n in enumerate(names)}
    alpha = (2 * depth) ** 0.25
    lam_inits = tuple(0.8 - 0.6 * math.exp(-0.3 * l) for l in range(depth))

    mod_rows = -(-(1 + dec_batch) // 8) * 8
    cvec = jnp.zeros((mod_rows, D), F32).at[0].set(c_ctx).at[1:1 + dec_batch].set(c)
    mod = _ada_mod(cvec, w_ada, b_ada).reshape(depth * mod_rows, 1, 3, D)
    lam_vec = _diff_lambdas(lambda_q1, lambda_k1, lambda_q2, lambda_k2, lam_inits)

    w_in_b = w_in.astype(BF16)
    w_s_b = w_spatial.astype(BF16)
    b_s_full = jnp.repeat(jnp.swapaxes(b_spatial, 1, 2), SG_GROUP_W, axis=2)
    wmg_b, wa_b, wb_b, wc_b, wo_b = (w.astype(BF16) for w in (w_mgate, w_br_a, w_br_b, w_br_c, w_out))
    rope_tabs = _rope_tables(dec_seq)
    bias_tab = _na_bias_tables(na_rel_bias)

    xp = x_prompt.reshape(1, batch * seq, D)
    xs = x_sample
    new_cache = (jnp.zeros((batch, depth, 2, h_b, seq, HEAD_DIM), F32),
                 jnp.zeros((batch, depth, h_b, seq, 2 * HEAD_DIM), F32),
                 jnp.zeros((batch, depth, h_c, seq, HEAD_DIM), F32),
                 jnp.zeros((batch, depth, h_c, seq, HEAD_DIM), F32))
    for l in range(depth):
        row2 = lambda a: a[l].reshape(1, -1)
        merge_w = (wmg_b[l], row2(b_mgate), wa_b[l], wb_b[l], wc_b[l], wo_b[l], row2(ln_g), row2(ln_b))
        sg = (row2(sg_norm_g), row2(sg_norm_b), w_s_b[l], b_s_full[l])

        h, z, *new_cache = _inproj(xp, mod, l * mod_rows, w_in_b[l], blk, tn, new_cache=new_cache, layer=l)
        zc = z.reshape(batch, seq, d_in)
        yb, yc = _ctx_attn(zc, blk, d_b, d_c, lam_vec, row2(diff_subln_g), l, lam_inits[l])
        flat = lambda a: a.reshape(1, batch * seq, a.shape[-1])
        xp = _merge(xp, h, z, blk, flat(yb), flat(yc), mod, l * mod_rows, sg, *merge_w, alpha)

        h, z, vt = _inproj(xs, mod, l * mod_rows + 1, w_in_b[l], blk, tn, rope_tabs=rope_tabs)
        yb = _diff_attn(z, vt, blk, d_b, lam_vec, row2(diff_subln_g), l, lam_inits[l],
                        cache=(cache_diff_k, cache_diff_v))
        yc = _na_attn(z, blk, d_c, cache_na_k, cache_na_v, bias_tab, l)
        xs = _merge(xs, h, z, blk, yb, yc, mod, l * mod_rows + 1, sg, *merge_w, alpha)

    return (xp.reshape(batch, seq, D), xs, *new_cache)
```

```python
import functools
import math

import jax
import jax.numpy as jnp
from jax import lax
from jax.experimental import pallas as pl
from jax.experimental.pallas import tpu as pltpu

GRID_W = 64
CHUNK = 128
HEAD_DIM = 64
SG_GROUP_W = 128
WIN_R = 8
WIN_W = 16
DA_TQ = 256
DA_TK = 256
DA_SUBTILES = 2
DA_ONES_ROWS = 16
INPROJ_SUBTILES = 2
MERGE_TM = 512
MERGE_SUBTILES = 2
NA_ROWS_PER_BLOCK = 4
NA_BLOCK_UNROLL = 8
ROPE_THETA = 10000.0
EPS = 1e-6
NEG_INF = -1e30
LANES = 128
LOG2E = math.log2(math.e)
Q_SCALE = HEAD_DIM ** -0.5 * LOG2E

BF16 = jnp.bfloat16
F32 = jnp.float32

VMEM_LIMIT = 56 * 1024 * 1024


def _cparams(sem):
    return pltpu.CompilerParams(dimension_semantics=sem, vmem_limit_bytes=VMEM_LIMIT)


def _sigmoid(x):
    return 1.0 / (1.0 + jnp.exp(-x))


def _silu(x):
    return x * _sigmoid(x)


def _dot(a, b):
    return jnp.dot(a, b, preferred_element_type=F32)


def _dot_nt(a, b):
    return lax.dot_general(a, b, (((1,), (1,)), ((), ())), preferred_element_type=F32)


def _layernorm(x):
    mu = jnp.mean(x, axis=-1, keepdims=True)
    xc = x - mu
    var = jnp.mean(xc * xc, axis=-1, keepdims=True)
    return xc * lax.rsqrt(var + EPS)


def _ada_kernel(c_ref, w_ref, b_ref, o_ref):
    s = _silu(c_ref[...]).astype(BF16)
    o_ref[0] = _dot(s, w_ref[0].astype(BF16)) + b_ref[0]


def _ada_mod(cvec, w_ada, b_ada):
    depth, d, d3 = w_ada.shape
    rows = cvec.shape[0]
    tn = 1024
    return pl.pallas_call(
        _ada_kernel,
        grid=(depth, d3 // tn),
        in_specs=[
            pl.BlockSpec((rows, d), lambda l, j: (0, 0)),
            pl.BlockSpec((1, d, tn), lambda l, j: (l, 0, j)),
            pl.BlockSpec((1, 1, tn), lambda l, j: (l, 0, j)),
        ],
        out_specs=pl.BlockSpec((1, rows, tn), lambda l, j: (l, 0, j)),
        out_shape=jax.ShapeDtypeStruct((depth, rows, d3), F32),
        compiler_params=_cparams(("parallel", "parallel")),
        name="ada_mod",
    )(cvec, w_ada, b_ada.reshape(depth, 1, d3))


def _lambda_kernel(lam_ref, o_ref, *, lam_inits):
    p = lam_ref[...]
    t1 = jnp.sum(p[:, 0, :] * p[:, 1, :], axis=-1, keepdims=True)
    t2 = jnp.sum(p[:, 2, :] * p[:, 3, :], axis=-1, keepdims=True)
    layer = lax.broadcasted_iota(jnp.int32, t1.shape, 0)
    init = jnp.zeros_like(t1)
    for l, v in enumerate(lam_inits):
        init = jnp.where(layer == l, v, init)
    o_ref[...] = jnp.broadcast_to(jnp.exp(t1) - jnp.exp(t2) + init, o_ref.shape)


def _diff_lambdas(lq1, lk1, lq2, lk2, lam_inits):
    depth = lq1.shape[0]
    params = jnp.stack([lq1, lk1, lq2, lk2], axis=1).astype(F32)
    out = pl.pallas_call(
        functools.partial(_lambda_kernel, lam_inits=lam_inits),
        out_shape=jax.ShapeDtypeStruct((depth, LANES), F32),
        name="diff_lambda",
    )(params)
    return out[:, 0]


def _rope(a, c, s_up, s_dn):
    outs = []
    for i in range(a.shape[1] // LANES):
        g = a[:, i * LANES:(i + 1) * LANES]
        outs.append(g * c + pltpu.roll(g, LANES - 16, 1) * s_up + pltpu.roll(g, 16, 1) * s_dn)
    return jnp.concatenate(outs, axis=1)


def _store_heads(out_ref, lead, rows, acc):
    width = out_ref.shape[-1]
    for hd in range(out_ref.shape[-3]):
        out_ref[(*lead, hd, rows, slice(None))] = acc[:, hd * width:(hd + 1) * width]


def _inproj_kernel(*refs, rope, blk, tn, ts):
    if rope:
        x_ref, mod_ref, w_ref, c_ref, su_ref, sd_ref, h_ref, z_ref, vt_ref = refs
    else:
        x_ref, mod_ref, w_ref, _, _, _, _, h_ref, z_ref, dk_ref, dv_ref, nk_ref, nv_ref = refs
    for sub in range(x_ref.shape[1] // ts):
        rows = slice(sub * ts, (sub + 1) * ts)
        xn = _layernorm(x_ref[0, rows, :])
        hb = (xn * (1.0 + mod_ref[0, 0, 1:2, :]) + mod_ref[0, 0, 0:1, :]).astype(BF16)
        h_ref[0, rows, :] = hb
        for j in range(w_ref.shape[1] // tn):
            cols = slice(j * tn, (j + 1) * tn)
            acc = _dot(hb, w_ref[:, cols])
            if rope and j in (blk["q_b"], blk["k_b"]):
                acc = _rope(acc, c_ref[rows, :], su_ref[rows, :], sd_ref[rows, :])
            if j in (blk["q_b"], blk["q_c"]):
                acc = acc * Q_SCALE
            z_ref[0, rows, cols] = acc.astype(z_ref.dtype)
            if rope and j == blk["v_b"]:
                vt_ref[0, :, rows] = acc.T.astype(vt_ref.dtype)
            if not rope:
                if j == blk["k_b"]:
                    for m in range(dk_ref.shape[2]):
                        half = dk_ref.shape[3] * dk_ref.shape[5]
                        _store_heads(dk_ref, (0, 0, m), rows, acc[:, m * half:(m + 1) * half])
                elif j == blk["v_b"]:
                    _store_heads(dv_ref, (0, 0), rows, acc)
                elif j == blk["k_c"]:
                    _store_heads(nk_ref, (0, 0), rows, acc)
                elif j == blk["v_c"]:
                    _store_heads(nv_ref, (0, 0), rows, acc)


def _inproj(x, mod, mod_row0, w, blk, tn, rope_tabs=None, new_cache=None, layer=None):
    B, L, D = x.shape
    d_in = w.shape[1]
    rope = rope_tabs is not None
    tm = min(L, 512) if rope else new_cache[0].shape[-2]
    in_specs = [
        pl.BlockSpec((1, tm, D), lambda b, i: (b, i, 0)),
        pl.BlockSpec((1, 1, 3, D), lambda b, i: (mod_row0 + b, 0, 0, 0)),
        pl.BlockSpec((D, d_in), lambda b, i: (0, 0), pipeline_mode=pl.Buffered(1)),
    ]
    args = [x, mod, w]
    out_specs = [
        pl.BlockSpec((1, tm, D), lambda b, i: (b, i, 0)),
        pl.BlockSpec((1, tm, d_in), lambda b, i: (b, i, 0)),
    ]
    out_shape = [
        jax.ShapeDtypeStruct((B, L, D), BF16),
        jax.ShapeDtypeStruct((B, L, d_in), BF16 if rope else F32),
    ]
    aliases = {}
    if rope:
        in_specs += [pl.BlockSpec((tm, LANES), lambda b, i: (i, 0))] * 3
        args += list(rope_tabs)
        out_specs.append(pl.BlockSpec((1, tn, tm), lambda b, i: (b, 0, i)))
        out_shape.append(jax.ShapeDtypeStruct((B, tn, L), BF16))
    else:
        assert B == 1 and L == new_cache[0].shape[0] * tm and tm % INPROJ_SUBTILES == 0
        for a in new_cache:
            aliases[len(args)] = len(out_shape)
            in_specs.append(pl.BlockSpec(memory_space=pl.ANY))
            args.append(a)
            tail = a.shape[2:]
            out_specs.append(pl.BlockSpec((1, 1, *tail), lambda b, i, n=len(tail): (i, layer) + (0,) * n))
            out_shape.append(jax.ShapeDtypeStruct(a.shape, a.dtype))
    return pl.pallas_call(
        functools.partial(_inproj_kernel, rope=rope, blk=blk, tn=tn, ts=tm // INPROJ_SUBTILES),
        grid=(B, L // tm),
        in_specs=in_specs,
        out_specs=out_specs,
        out_shape=out_shape,
        input_output_aliases=aliases,
        compiler_params=_cparams(("parallel", "parallel")),
        name="inproj_rope" if rope else "inproj",
    )(*args)


def _head_lanes(hh):
    lane = lax.broadcasted_iota(jnp.int32, (1, LANES), 1)
    return (lane < HEAD_DIM) if hh == 0 else (lane >= HEAD_DIM)


def _pair_lanes(ref):
    return jnp.concatenate([ref[0], ref[1]], axis=-1).astype(BF16)


def _diff_attn_kernel(lam_ref, q1_ref, q2_ref, k1_ref, k2_ref, vt_ref, g_ref, kc1_ref, kc2_ref, vc_ref, sg_ref,
                      o_ref, *, layer, lam_init, tq, tk, n_self, n_ctx):
    lam = lam_ref[layer]
    dv = 2 * HEAD_DIM
    ones = jnp.ones((DA_ONES_ROWS, tk), BF16)
    kc = [_pair_lanes(r.at[0, 0, 0]) for r in (kc1_ref, kc2_ref)]
    vct = jnp.concatenate([vc_ref[0, 0, hd].T for hd in range(2)], axis=0).astype(BF16)
    chunks = [(k1_ref.at[0, c * tk:(c + 1) * tk, :], k2_ref.at[0, c * tk:(c + 1) * tk, :],
               vt_ref.at[0, :, c * tk:(c + 1) * tk]) for c in range(n_self)]
    chunks += [(kc[0][c * tk:(c + 1) * tk], kc[1][c * tk:(c + 1) * tk], vct[:, c * tk:(c + 1) * tk])
               for c in range(n_ctx)]
    units = [(sub, hh, m) for sub in range(q1_ref.shape[1] // tq) for hh in range(2) for m in range(2)]
    qm, state = {}, {}
    for u in units:
        sub, hh, m = u
        q = (q1_ref, q2_ref)[m][0, sub * tq:(sub + 1) * tq, :]
        qm[u] = jnp.where(_head_lanes(hh), q, jnp.zeros_like(q)).astype(BF16)
        state[u] = (jnp.full((1, tq), -jnp.inf, F32), jnp.zeros((dv + DA_ONES_ROWS, tq), F32))

    def scores(u, chunk):
        return _dot_nt(chunk[u[2]][...], qm[u])

    def update(u, chunk, st):
        hh = u[1]
        vt_aug = jnp.concatenate([chunk[2][hh * dv:(hh + 1) * dv, :], ones], axis=0)
        mx, acc = state[u]
        mx_new = jnp.maximum(mx, jnp.max(st, axis=0, keepdims=True))
        pt = jnp.exp2(st - mx_new).astype(BF16)
        state[u] = (mx_new, jnp.exp2(mx - mx_new) * acc + _dot(vt_aug, pt))

    cur = {u: scores(u, chunks[0]) for u in units}
    for c, chunk in enumerate(chunks):
        nxt = {}
        for u in units:
            if c + 1 < len(chunks):
                nxt[u] = scores(u, chunks[c + 1])
            update(u, chunk, cur[u])
        cur = nxt

    for sub in range(q1_ref.shape[1] // tq):
        rows = slice(sub * tq, (sub + 1) * tq)
        for hh in range(2):
            cols = slice(hh * dv, (hh + 1) * dv)
            a1, a2 = state[sub, hh, 0][1], state[sub, hh, 1][1]
            ot = a1[:dv] * (1.0 / a1[dv:dv + 1]) - a2[:dv] * (lam / a2[dv:dv + 1])
            oft = ot * lax.rsqrt(jnp.mean(ot * ot, axis=0, keepdims=True) + EPS)
            y = oft.T * sg_ref[...] * (1.0 - lam_init) * _silu(g_ref[0, rows, cols].astype(F32))
            o_ref[0, rows, cols] = y.astype(o_ref.dtype)


def _diff_attn(z, vt, blk, d_b, lam_vec, subln_g, layer, lam_init, cache):
    B, L, _ = z.shape
    tq = min(L, DA_TQ)
    bq = min(L, DA_SUBTILES * tq)
    tk = min(L, DA_TK)
    n_pairs = d_b // (4 * HEAD_DIM)
    col128 = lambda name: blk[name] * (d_b // LANES)
    col256 = lambda name: blk[name] * (d_b // (2 * LANES))
    qspec = lambda m: pl.BlockSpec((1, bq, LANES), lambda b, p, i: (b, i, col128("q_b") + m * n_pairs + p))
    kspec = lambda m: pl.BlockSpec((1, L, LANES), lambda b, p, i: (b, 0, col128("k_b") + m * n_pairs + p))
    ck, cv = cache
    P = ck.shape[4]
    assert P % tk == 0 and L % tk == 0 and L % bq == 0
    kcspec = lambda m: pl.BlockSpec((1, 1, 1, 2, P, HEAD_DIM), lambda b, p, i: (b, layer, m, p, 0, 0))
    in_specs = [
        pl.BlockSpec(memory_space=pltpu.SMEM),
        qspec(0), qspec(1), kspec(0), kspec(1),
        pl.BlockSpec((1, 2 * LANES, L), lambda b, p, i: (b, p, 0)),
        pl.BlockSpec((1, bq, 2 * LANES), lambda b, p, i: (b, i, col256("g_b") + p)),
        kcspec(0), kcspec(1),
        pl.BlockSpec((1, 1, 2, P, 2 * HEAD_DIM), lambda b, p, i: (b, layer, p, 0, 0)),
        pl.BlockSpec((1, 2 * HEAD_DIM), lambda b, p, i: (0, 0)),
    ]
    return pl.pallas_call(
        functools.partial(_diff_attn_kernel, layer=layer, lam_init=lam_init, tq=tq, tk=tk, n_self=L // tk,
                          n_ctx=P // tk),
        grid=(B, n_pairs, L // bq),
        in_specs=in_specs,
        out_specs=pl.BlockSpec((1, bq, 2 * LANES), lambda b, p, i: (b, i, p)),
        out_shape=jax.ShapeDtypeStruct((B, L, d_b), BF16),
        compiler_params=_cparams(("parallel", "parallel", "parallel")),
        name="diff_attn_latent",
    )(lam_vec, z, z, z, z, vt, z, ck, ck, cv, subln_g)


def _ctx_attn_kernel(lam_ref, qb_ref, kb_ref, vb_ref, gb_ref, qc_ref, kc_ref, vc_ref, gc_ref, sg_ref,
                     yb_ref, yc_ref, *, layer, lam_init):
    dv = 2 * HEAD_DIM
    L = qb_ref.shape[1]
    lam = lam_ref[layer]
    n_pairs_b = qb_ref.shape[2] // (2 * LANES)
    n_pairs_c = qc_ref.shape[2] // LANES
    ones = jnp.ones((L, LANES), BF16)

    def masked(q, hh):
        return jnp.where(_head_lanes(hh), q, jnp.zeros_like(q)).astype(BF16)

    s_b, s_c = {}, {}
    for p in range(n_pairs_b):
        for m in range(2):
            cols = slice((m * n_pairs_b + p) * LANES, (m * n_pairs_b + p + 1) * LANES)
            k = kb_ref[0, :, cols].astype(BF16)
            for hh in range(2):
                s_b[p, hh, m] = _dot_nt(masked(qb_ref[0, :, cols], hh), k)
    for p in range(n_pairs_c):
        cols = slice(p * LANES, (p + 1) * LANES)
        k = kc_ref[0, :, cols].astype(BF16)
        for hh in range(2):
            s_c[p, hh] = _dot_nt(masked(qc_ref[0, :, cols], hh), k)

    def attend(s, vaug):
        o = _dot(jnp.exp2(s - jnp.max(s, axis=-1, keepdims=True)).astype(BF16), vaug)
        return o[:, :vaug.shape[1] - LANES] * (1.0 / o[:, vaug.shape[1] - LANES:vaug.shape[1] - LANES + 1])

    for p in range(n_pairs_b):
        for hh in range(2):
            cols = slice((2 * p + hh) * dv, (2 * p + hh + 1) * dv)
            vaug = jnp.concatenate([vb_ref[0, :, cols].astype(BF16), ones], axis=1)
            o = attend(s_b[p, hh, 0], vaug) - lam * attend(s_b[p, hh, 1], vaug)
            of = o * lax.rsqrt(jnp.mean(o * o, axis=-1, keepdims=True) + EPS)
            y = of * sg_ref[...] * (1.0 - lam_init) * _silu(gb_ref[0, :, cols].astype(F32))
            yb_ref[0, :, cols] = y.astype(yb_ref.dtype)

    lo = _head_lanes(0)
    for p in range(n_pairs_c):
        cols = slice(p * LANES, (p + 1) * LANES)
        vaug = jnp.concatenate([vc_ref[0, :, cols].astype(BF16), ones], axis=1)
        o = jnp.where(lo, attend(s_c[p, 0], vaug), attend(s_c[p, 1], vaug))
        yc_ref[0, :, cols] = (o * _silu(gc_ref[0, :, cols].astype(F32))).astype(yc_ref.dtype)


def _ctx_attn(z, blk, d_b, d_c, lam_vec, subln_g, layer, lam_init):
    B, L, _ = z.shape
    assert d_b == d_c
    spec = lambda name: pl.BlockSpec((1, L, d_b), lambda b, c=blk[name]: (b, 0, c))
    names = ("q_b", "k_b", "v_b", "g_b", "q_c", "k_c", "v_c", "g_c")
    return pl.pallas_call(
        functools.partial(_ctx_attn_kernel, layer=layer, lam_init=lam_init),
        grid=(B,),
        in_specs=[pl.BlockSpec(memory_space=pltpu.SMEM), *[spec(n) for n in names],
                  pl.BlockSpec((1, 2 * HEAD_DIM), lambda b: (0, 0))],
        out_specs=[pl.BlockSpec((1, L, d_b), lambda b: (b, 0, 0)), pl.BlockSpec((1, L, d_c), lambda b: (b, 0, 0))],
        out_shape=[jax.ShapeDtypeStruct((B, L, d_b), BF16), jax.ShapeDtypeStruct((B, L, d_c), BF16)],
        compiler_params=_cparams(("parallel",)),
        name="ctx_attn",
    )(lam_vec, *([z] * len(names)), subln_g)


def _na_kernel(q_ref, k_ref, v_ref, g_ref, kc_ref, vc_ref, bias_ref, o_ref, *, rows):
    lane = lax.broadcasted_iota(jnp.int32, (1, LANES), 1)
    n_loc = WIN_R * GRID_W
    P = kc_ref.shape[3]
    kc = _pair_lanes(kc_ref.at[0, 0])
    vc_aug = jnp.concatenate([_pair_lanes(vc_ref.at[0, 0]), jnp.ones((P, LANES), BF16)], axis=1)
    ones_loc = jnp.ones((n_loc, LANES), BF16)
    tq = NA_ROWS_PER_BLOCK * GRID_W
    lo = lane < HEAD_DIM

    def block(i, carry):
        q0 = pl.multiple_of(i * tq, tq)
        q = q_ref[0, pl.ds(q0, tq), :]
        zero = jnp.zeros_like(q)
        q_heads = (jnp.where(lo, q, zero), jnp.where(lo, zero, q))
        q2 = jnp.concatenate([qh[j * GRID_W:(j + 1) * GRID_W] for j in range(NA_ROWS_PER_BLOCK) for qh in q_heads],
                             axis=0)
        s_ctx = _dot_nt(q2, kc)
        m_ctx = jnp.max(s_ctx, axis=-1, keepdims=True)
        o_rows, p_ctx_rows = [], []
        for j in range(NA_ROWS_PER_BLOCK):
            r = i * NA_ROWS_PER_BLOCK + j
            rs = jnp.clip(r - WIN_R // 2, 0, rows - WIN_R)
            off = rs - r + WIN_R - 1
            k0 = pl.multiple_of(rs * GRID_W, GRID_W)
            kl = k_ref[0, pl.ds(k0, n_loc), :]
            vl_aug = jnp.concatenate([v_ref[0, pl.ds(k0, n_loc), :], ones_loc], axis=1)
            rsl = slice(2 * j * GRID_W, 2 * (j + 1) * GRID_W)
            s_loc = _dot_nt(q2[rsl], kl) + bias_ref[0, :, off].reshape(2 * GRID_W, n_loc)
            m = jnp.maximum(jnp.max(s_loc, axis=-1, keepdims=True), m_ctx[rsl])
            p_ctx_rows.append(jnp.exp2(s_ctx[rsl] - m).astype(BF16))
            o_rows.append(_dot(jnp.exp2(s_loc - m).astype(BF16), vl_aug))
        o = jnp.concatenate(o_rows, axis=0) + _dot(jnp.concatenate(p_ctx_rows, axis=0), vc_aug)
        o = o[:, :LANES] * (1.0 / o[:, LANES:LANES + 1])
        o = jnp.concatenate(
            [jnp.where(lo, o[2 * j * GRID_W:(2 * j + 1) * GRID_W], o[(2 * j + 1) * GRID_W:(2 * j + 2) * GRID_W])
             for j in range(NA_ROWS_PER_BLOCK)], axis=0)
        y = o * _silu(g_ref[0, pl.ds(q0, tq), :].astype(F32))
        o_ref[0, pl.ds(q0, tq), :] = y.astype(o_ref.dtype)
        return carry

    n_blocks = rows // NA_ROWS_PER_BLOCK
    lax.fori_loop(0, n_blocks, block, 0, unroll=math.gcd(n_blocks, NA_BLOCK_UNROLL))


def _na_attn(z, blk, d_c, nk, nv, bias_tab, layer):
    B, L, _ = z.shape
    rows = L // GRID_W
    assert rows >= WIN_R and L % GRID_W == 0 and rows % NA_ROWS_PER_BLOCK == 0
    P = nk.shape[3]
    n_pairs = d_c // LANES
    spec = lambda name: pl.BlockSpec((1, L, LANES), lambda b, p, c=blk[name] * n_pairs: (b, 0, c + p))
    cspec = pl.BlockSpec((1, 1, 2, P, HEAD_DIM), lambda b, p: (b, layer, p, 0, 0))
    return pl.pallas_call(
        functools.partial(_na_kernel, rows=rows),
        grid=(B, n_pairs),
        in_specs=[
            spec("q_c"), spec("k_c"), spec("v_c"), spec("g_c"), cspec, cspec,
            pl.BlockSpec((1, 2, WIN_R, GRID_W, WIN_R * GRID_W), lambda b, p: (layer, p, 0, 0, 0)),
        ],
        out_specs=pl.BlockSpec((1, L, LANES), lambda b, p: (b, 0, p)),
        out_shape=jax.ShapeDtypeStruct((B, L, d_c), BF16),
        compiler_params=_cparams(("parallel", "parallel")),
        name="na_attn_latent",
    )(z, z, z, z, nk, nv, bias_tab)


def _spatial_gate(u, v, g, ng_ref, nb_ref, ws_ref, bs_ref):
    n_groups = v.shape[1] // SG_GROUP_W
    vn = (_layernorm(v.astype(F32)) * ng_ref[...] + nb_ref[...]).astype(BF16)
    ys = []
    for n in range(v.shape[0] // CHUNK):
        rows = slice(n * CHUNK, (n + 1) * CHUNK)
        sv = jnp.concatenate(
            [_dot(ws_ref[gi], vn[rows, gi * SG_GROUP_W:(gi + 1) * SG_GROUP_W]) for gi in range(n_groups)],
            axis=1) + bs_ref[...]
        ys.append((u[rows].astype(F32) * sv * _silu(g[rows].astype(F32))).astype(BF16))
    return jnp.concatenate(ys, axis=0)


def _merge_kernel(x_ref, h_ref, u_ref, v_ref, g_ref, yb_ref, yc_ref, mod_ref, ng_ref, nb_ref, ws_ref, bs_ref,
                  wmg_ref, bmg_ref, wa_ref, wb_ref, wc_ref, wo_ref, lg_ref, lb_ref, o_ref, *, alpha, ts):
    D = x_ref.shape[2]
    for sub in range(x_ref.shape[1] // ts):
        rows = slice(sub * ts, (sub + 1) * ts)
        h = h_ref[0, rows, :]
        ya = _spatial_gate(u_ref[0, rows, :], v_ref[0, rows, :], g_ref[0, rows, :], ng_ref, nb_ref, ws_ref, bs_ref)
        m = None
        for i, (y, w_ref) in enumerate(((ya, wa_ref), (yb_ref[0, rows, :], wb_ref), (yc_ref[0, rows, :], wc_ref))):
            gate = _sigmoid(_dot(h, wmg_ref[:, i * D:(i + 1) * D]) + bmg_ref[:, i * D:(i + 1) * D])
            t = gate * _dot(y, w_ref[...])
            m = t if m is None else m + t
        out = _dot(m.astype(BF16), wo_ref[...])
        t = alpha * x_ref[0, rows, :] + mod_ref[0, 0, 2:3, :] * out
        o_ref[0, rows, :] = _layernorm(t) * lg_ref[...] + lb_ref[...]


def _merge(x, h, z, blk, yb, yc, mod, mod_row0, sg, wmg, bmg, wa, wb, wc, wo, lg, lb, alpha):
    B, L, D = x.shape
    d_a = wa.shape[0]
    tm = min(L, MERGE_TM)
    ts = tm // MERGE_SUBTILES
    assert ts % CHUNK == 0 and L % tm == 0
    tok = lambda w: pl.BlockSpec((1, tm, w), lambda b, i: (b, i, 0))
    zspec = lambda name: pl.BlockSpec((1, tm, d_a), lambda b, i, c=blk[name]: (b, i, c))
    const = lambda a: pl.BlockSpec(a.shape, lambda b, i: (0,) * a.ndim, pipeline_mode=pl.Buffered(1))
    weights = (*sg, wmg, bmg, wa, wb, wc, wo, lg, lb)
    return pl.pallas_call(
        functools.partial(_merge_kernel, alpha=alpha, ts=ts),
        grid=(B, L // tm),
        in_specs=[
            tok(D), tok(D), zspec("u_a"), zspec("v_a"), zspec("g_a"), tok(yb.shape[2]), tok(yc.shape[2]),
            pl.BlockSpec((1, 1, 3, D), lambda b, i: (mod_row0 + b, 0, 0, 0)),
            *[const(w) for w in weights],
        ],
        out_specs=tok(D),
        out_shape=jax.ShapeDtypeStruct((B, L, D), F32),
        compiler_params=_cparams(("parallel", "parallel")),
        name="merge_postnorm",
    )(x, h, z, z, z, yb, yc, mod, *weights)


def _rope_tables(n_tok):
    half = HEAD_DIM // 2
    quarter = HEAD_DIM // 4
    t = jnp.arange(n_tok)
    inv = 1.0 / (ROPE_THETA ** (jnp.arange(0, half, 2, dtype=F32) / half))
    ang = jnp.stack([(t // GRID_W).astype(F32)[:, None] * inv, (t % GRID_W).astype(F32)[:, None] * inv], axis=1)
    cos = jnp.cos(ang)[:, :, None, :]
    sin = jnp.sin(ang)[:, :, None, :]
    zero = jnp.zeros_like(sin)
    shape = (n_tok, 2, 2, quarter)
    c = jnp.broadcast_to(cos, shape).reshape(n_tok, HEAD_DIM)
    s_up = jnp.concatenate([-sin, zero], axis=2).reshape(n_tok, HEAD_DIM)
    s_dn = jnp.concatenate([zero, sin], axis=2).reshape(n_tok, HEAD_DIM)
    rep = LANES // HEAD_DIM
    return tuple(jnp.tile(a, (1, rep)) for a in (c, s_up, s_dn))


def _na_bias_kernel(rb_ref, o_ref):
    shape = (GRID_W, LANES)
    qcol = lax.broadcasted_iota(jnp.int32, shape, 0)
    lane = lax.broadcasted_iota(jnp.int32, shape, 1)
    kcol = lane & (GRID_W - 1)
    cstart = jnp.clip(qcol - WIN_W // 2, 0, GRID_W - WIN_W)
    in_window = (kcol >= cstart) & (kcol < cstart + WIN_W)

    def toeplitz(y, lane0):
        row = jnp.broadcast_to(rb_ref[0, y:y + 1, :], shape)
        return pltpu.roll(row, (lane0 - (WIN_W - 1)) % LANES, 1, stride=1, stride_axis=0)

    n_dy = 2 * WIN_R - 1
    t_lo = [toeplitz(y, 0) for y in range(n_dy)]
    t_hi = [toeplitz(y, GRID_W) for y in range(n_dy)]
    for off in range(WIN_R):
        for jp in range(WIN_R // 2):
            y = off + 2 * jp
            tile = jnp.where(lane < GRID_W, t_lo[y], t_hi[y + 1]) * LOG2E
            o_ref[0, 0, off, :, jp * LANES:(jp + 1) * LANES] = jnp.where(in_window, tile, NEG_INF)


def _na_bias_tables(rel_bias):
    depth, n_heads, n_dy, n_dx = rel_bias.shape
    assert n_dy == 2 * WIN_R - 1 and n_dx == 2 * WIN_W - 1 and WIN_R % 2 == 0
    rb = jnp.pad(rel_bias.astype(F32), ((0, 0), (0, 0), (0, 16 - n_dy), (0, LANES - n_dx)))
    return pl.pallas_call(
        _na_bias_kernel,
        grid=(depth, n_heads),
        in_specs=[pl.BlockSpec((1, 16, LANES), lambda l, h: (l * n_heads + h, 0, 0))],
        out_specs=pl.BlockSpec((1, 1, WIN_R, GRID_W, WIN_R * GRID_W), lambda l, h: (l, h, 0, 0, 0)),
        out_shape=jax.ShapeDtypeStruct((depth, n_heads, WIN_R, GRID_W, WIN_R * GRID_W), F32),
        compiler_params=_cparams(("parallel", "parallel")),
        name="na_bias_table",
    )(rb.reshape(depth * n_heads, 16, LANES))


def kernel(x_prompt, x_sample, c, cache_diff_k, cache_diff_v, cache_na_k, cache_na_v, c_ctx, w_ada, b_ada, w_in, sg_norm_g, sg_norm_b, w_spatial, b_spatial, lambda_q1, lambda_k1, lambda_q2, lambda_k2, diff_subln_g, na_rel_bias, w_br_a, w_br_b, w_br_c, w_mgate, b_mgate, w_out, ln_g, ln_b):
    depth, D, d_in = w_in.shape
    batch, seq, _ = x_prompt.shape
    dec_batch, dec_seq, _ = x_sample.shape
    d_a, d_b, d_c = w_br_a.shape[1], w_br_b.shape[1], w_br_c.shape[1]
    h_b = d_b // (2 * HEAD_DIM)
    h_c = d_c // HEAD_DIM
    tn = d_a
    assert d_a == d_b == d_c and d_in == 11 * tn and tn % (2 * LANES) == 0
    assert d_a // SG_GROUP_W == w_spatial.shape[1] and w_spatial.shape[2] == CHUNK
    names = ("u_a", "v_a", "g_a", "q_b", "k_b", "v_b", "g_b", "q_c", "k_c", "v_c", "g_c")
    blk = {n: i for i, n in enumerate(names)}
    alpha = (2 * depth) ** 0.25
    lam_inits = tuple(0.8 - 0.6 * math.exp(-0.3 * l) for l in range(depth))

    mod_rows = -(-(1 + dec_batch) // 8) * 8
    cvec = jnp.zeros((mod_rows, D), F32).at[0].set(c_ctx).at[1:1 + dec_batch].set(c)
    mod = _ada_mod(cvec, w_ada, b_ada).reshape(depth * mod_rows, 1, 3, D)
    lam_vec = _diff_lambdas(lambda_q1, lambda_k1, lambda_q2, lambda_k2, lam_inits)

    w_in_b = w_in.astype(BF16)
    w_s_b = w_spatial.astype(BF16)
    b_s_full = jnp.repeat(jnp.swapaxes(b_spatial, 1, 2), SG_GROUP_W, axis=2)
    wmg_b, wa_b, wb_b, wc_b, wo_b = (w.astype(BF16) for w in (w_mgate, w_br_a, w_br_b, w_br_c, w_out))
    rope_tabs = _rope_tables(dec_seq)
    bias_tab = _na_bias_tables(na_rel_bias)

    xp = x_prompt.reshape(1, batch * seq, D)
    xs = x_sample
    new_cache = (jnp.zeros((batch, depth, 2, h_b, seq, HEAD_DIM), F32),
                 jnp.zeros((batch, depth, h_b, seq, 2 * HEAD_DIM), F32),
                 jnp.zeros((batch, depth, h_c, seq, HEAD_DIM), F32),
                 jnp.zeros((batch, depth, h_c, seq, HEAD_DIM), F32))
    for l in range(depth):
        row2 = lambda a: a[l].reshape(1, -1)
        merge_w = (wmg_b[l], row2(b_mgate), wa_b[l], wb_b[l], wc_b[l], wo_b[l], row2(ln_g), row2(ln_b))
        sg = (row2(sg_norm_g), row2(sg_norm_b), w_s_b[l], b_s_full[l])

        h, z, *new_cache = _inproj(xp, mod, l * mod_rows, w_in_b[l], blk, tn, new_cache=new_cache, layer=l)
        zc = z.reshape(batch, seq, d_in)
        yb, yc = _ctx_attn(zc, blk, d_b, d_c, lam_vec, row2(diff_subln_g), l, lam_inits[l])
        flat = lambda a: a.reshape(1, batch * seq, a.shape[-1])
        xp = _merge(xp, h, z, blk, flat(yb), flat(yc), mod, l * mod_rows, sg, *merge_w, alpha)

        h, z, vt = _inproj(xs, mod, l * mod_rows + 1, w_in_b[l], blk, tn, rope_tabs=rope_tabs)
        yb = _diff_attn(z, vt, blk, d_b, lam_vec, row2(diff_subln_g), l, lam_inits[l],
                        cache=(cache_diff_k, cache_diff_v))
        yc = _na_attn(z, blk, d_c, cache_na_k, cache_na_v, bias_tab, l)
        xs = _merge(xs, h, z, blk, yb, yc, mod, l * mod_rows + 1, sg, *merge_w, alpha)

    return (xp.reshape(batch, seq, D), xs, *new_cache)
```

```python
import functools
import math

import jax
import jax.numpy as jnp
from jax import lax
from jax.experimental import pallas as pl
from jax.experimental.pallas import tpu as pltpu

GRID_W = 64
CHUNK = 128
HEAD_DIM = 64
SG_GROUP_W = 128
WIN_R = 8
WIN_W = 16
DA_TQ = 256
DA_TK = 256
DA_SUBTILES = 2
DA_ONES_ROWS = 16
INPROJ_SUBTILES = 2
MERGE_TM = 512
MERGE_SUBTILES = 2
NA_ROWS_PER_BLOCK = 4
NA_BLOCK_UNROLL = 16
ROPE_THETA = 10000.0
EPS = 1e-6
NEG_INF = -1e30
LANES = 128
LOG2E = math.log2(math.e)
Q_SCALE = HEAD_DIM ** -0.5 * LOG2E

BF16 = jnp.bfloat16
F32 = jnp.float32

VMEM_LIMIT = 56 * 1024 * 1024


def _cparams(sem):
    return pltpu.CompilerParams(dimension_semantics=sem, vmem_limit_bytes=VMEM_LIMIT)


def _sigmoid(x):
    return 1.0 / (1.0 + jnp.exp(-x))


def _silu(x):
    return x * _sigmoid(x)


def _dot(a, b):
    return jnp.dot(a, b, preferred_element_type=F32)


def _dot_nt(a, b):
    return lax.dot_general(a, b, (((1,), (1,)), ((), ())), preferred_element_type=F32)


def _layernorm(x):
    mu = jnp.mean(x, axis=-1, keepdims=True)
    xc = x - mu
    var = jnp.mean(xc * xc, axis=-1, keepdims=True)
    return xc * lax.rsqrt(var + EPS)


def _ada_kernel(c_ref, w_ref, b_ref, o_ref):
    s = _silu(c_ref[...]).astype(BF16)
    o_ref[0] = _dot(s, w_ref[0].astype(BF16)) + b_ref[0]


def _ada_mod(cvec, w_ada, b_ada):
    depth, d, d3 = w_ada.shape
    rows = cvec.shape[0]
    tn = 1024
    return pl.pallas_call(
        _ada_kernel,
        grid=(depth, d3 // tn),
        in_specs=[
            pl.BlockSpec((rows, d), lambda l, j: (0, 0)),
            pl.BlockSpec((1, d, tn), lambda l, j: (l, 0, j)),
            pl.BlockSpec((1, 1, tn), lambda l, j: (l, 0, j)),
        ],
        out_specs=pl.BlockSpec((1, rows, tn), lambda l, j: (l, 0, j)),
        out_shape=jax.ShapeDtypeStruct((depth, rows, d3), F32),
        compiler_params=_cparams(("parallel", "parallel")),
        name="ada_mod",
    )(cvec, w_ada, b_ada.reshape(depth, 1, d3))


def _lambda_kernel(lam_ref, o_ref, *, lam_inits):
    p = lam_ref[...]
    t1 = jnp.sum(p[:, 0, :] * p[:, 1, :], axis=-1, keepdims=True)
    t2 = jnp.sum(p[:, 2, :] * p[:, 3, :], axis=-1, keepdims=True)
    layer = lax.broadcasted_iota(jnp.int32, t1.shape, 0)
    init = jnp.zeros_like(t1)
    for l, v in enumerate(lam_inits):
        init = jnp.where(layer == l, v, init)
    o_ref[...] = jnp.broadcast_to(jnp.exp(t1) - jnp.exp(t2) + init, o_ref.shape)


def _diff_lambdas(lq1, lk1, lq2, lk2, lam_inits):
    depth = lq1.shape[0]
    params = jnp.stack([lq1, lk1, lq2, lk2], axis=1).astype(F32)
    out = pl.pallas_call(
        functools.partial(_lambda_kernel, lam_inits=lam_inits),
        out_shape=jax.ShapeDtypeStruct((depth, LANES), F32),
        name="diff_lambda",
    )(params)
    return out[:, 0]


def _rope(a, c, s_up, s_dn):
    outs = []
    for i in range(a.shape[1] // LANES):
        g = a[:, i * LANES:(i + 1) * LANES]
        outs.append(g * c + pltpu.roll(g, LANES - 16, 1) * s_up + pltpu.roll(g, 16, 1) * s_dn)
    return jnp.concatenate(outs, axis=1)


def _store_heads(out_ref, lead, rows, acc):
    width = out_ref.shape[-1]
    for hd in range(out_ref.shape[-3]):
        out_ref[(*lead, hd, rows, slice(None))] = acc[:, hd * width:(hd + 1) * width]


def _inproj_kernel(*refs, rope, blk, tn, ts):
    if rope:
        x_ref, mod_ref, w_ref, c_ref, su_ref, sd_ref, h_ref, z_ref, vt_ref = refs
    else:
        x_ref, mod_ref, w_ref, _, _, _, _, h_ref, z_ref, dk_ref, dv_ref, nk_ref, nv_ref = refs
    for sub in range(x_ref.shape[1] // ts):
        rows = slice(sub * ts, (sub + 1) * ts)
        xn = _layernorm(x_ref[0, rows, :])
        hb = (xn * (1.0 + mod_ref[0, 0, 1:2, :]) + mod_ref[0, 0, 0:1, :]).astype(BF16)
        h_ref[0, rows, :] = hb
        for j in range(w_ref.shape[1] // tn):
            cols = slice(j * tn, (j + 1) * tn)
            acc = _dot(hb, w_ref[:, cols])
            if rope and j in (blk["q_b"], blk["k_b"]):
                acc = _rope(acc, c_ref[rows, :], su_ref[rows, :], sd_ref[rows, :])
            if j in (blk["q_b"], blk["q_c"]):
                acc = acc * Q_SCALE
            z_ref[0, rows, cols] = acc.astype(z_ref.dtype)
            if rope and j == blk["v_b"]:
                vt_ref[0, :, rows] = acc.T.astype(vt_ref.dtype)
            if not rope:
                if j == blk["k_b"]:
                    for m in range(dk_ref.shape[2]):
                        half = dk_ref.shape[3] * dk_ref.shape[5]
                        _store_heads(dk_ref, (0, 0, m), rows, acc[:, m * half:(m + 1) * half])
                elif j == blk["v_b"]:
                    _store_heads(dv_ref, (0, 0), rows, acc)
                elif j == blk["k_c"]:
                    _store_heads(nk_ref, (0, 0), rows, acc)
                elif j == blk["v_c"]:
                    _store_heads(nv_ref, (0, 0), rows, acc)


def _inproj(x, mod, mod_row0, w, blk, tn, rope_tabs=None, new_cache=None, layer=None):
    B, L, D = x.shape
    d_in = w.shape[2]
    rope = rope_tabs is not None
    tm = min(L, 512) if rope else new_cache[0].shape[-2]
    in_specs = [
        pl.BlockSpec((1, tm, D), lambda b, i: (b, i, 0)),
        pl.BlockSpec((1, 1, 3, D), lambda b, i: (mod_row0 + b, 0, 0, 0)),
        pl.BlockSpec((None, D, d_in), lambda b, i: (layer, 0, 0), pipeline_mode=pl.Buffered(1)),
    ]
    args = [x, mod, w]
    out_specs = [
        pl.BlockSpec((1, tm, D), lambda b, i: (b, i, 0)),
        pl.BlockSpec((1, tm, d_in), lambda b, i: (b, i, 0)),
    ]
    out_shape = [
        jax.ShapeDtypeStruct((B, L, D), BF16),
        jax.ShapeDtypeStruct((B, L, d_in), BF16 if rope else F32),
    ]
    aliases = {}
    if rope:
        in_specs += [pl.BlockSpec((tm, LANES), lambda b, i: (i, 0))] * 3
        args += list(rope_tabs)
        out_specs.append(pl.BlockSpec((1, tn, tm), lambda b, i: (b, 0, i)))
        out_shape.append(jax.ShapeDtypeStruct((B, tn, L), BF16))
    else:
        assert B == 1 and L == new_cache[0].shape[0] * tm and tm % INPROJ_SUBTILES == 0
        for a in new_cache:
            aliases[len(args)] = len(out_shape)
            in_specs.append(pl.BlockSpec(memory_space=pl.ANY))
            args.append(a)
            tail = a.shape[2:]
            out_specs.append(pl.BlockSpec((1, 1, *tail), lambda b, i, n=len(tail): (i, layer) + (0,) * n))
            out_shape.append(jax.ShapeDtypeStruct(a.shape, a.dtype))
    return pl.pallas_call(
        functools.partial(_inproj_kernel, rope=rope, blk=blk, tn=tn, ts=tm // INPROJ_SUBTILES),
        grid=(B, L // tm),
        in_specs=in_specs,
        out_specs=out_specs,
        out_shape=out_shape,
        input_output_aliases=aliases,
        compiler_params=_cparams(("parallel", "parallel")),
        name="inproj_rope" if rope else "inproj",
    )(*args)


def _head_lanes(hh):
    lane = lax.broadcasted_iota(jnp.int32, (1, LANES), 1)
    return (lane < HEAD_DIM) if hh == 0 else (lane >= HEAD_DIM)


def _pair_lanes(ref):
    return jnp.concatenate([ref[0], ref[1]], axis=-1).astype(BF16)


def _diff_attn_kernel(lam_ref, q1_ref, q2_ref, k1_ref, k2_ref, vt_ref, g_ref, kc1_ref, kc2_ref, vc_ref, sg_ref,
                      o_ref, *, layer, lam_init, tq, tk, n_self, n_ctx):
    lam = lam_ref[layer]
    dv = 2 * HEAD_DIM
    ones = jnp.ones((DA_ONES_ROWS, tk), BF16)
    kc = [_pair_lanes(r.at[0, 0, 0]) for r in (kc1_ref, kc2_ref)]
    vct = jnp.concatenate([vc_ref[0, 0, hd].T for hd in range(2)], axis=0).astype(BF16)
    chunks = [(k1_ref.at[0, c * tk:(c + 1) * tk, :], k2_ref.at[0, c * tk:(c + 1) * tk, :],
               vt_ref.at[0, :, c * tk:(c + 1) * tk]) for c in range(n_self)]
    chunks += [(kc[0][c * tk:(c + 1) * tk], kc[1][c * tk:(c + 1) * tk], vct[:, c * tk:(c + 1) * tk])
               for c in range(n_ctx)]
    units = [(sub, hh, m) for sub in range(q1_ref.shape[1] // tq) for hh in range(2) for m in range(2)]
    qm, state = {}, {}
    for u in units:
        sub, hh, m = u
        q = (q1_ref, q2_ref)[m][0, sub * tq:(sub + 1) * tq, :]
        qm[u] = jnp.where(_head_lanes(hh), q, jnp.zeros_like(q)).astype(BF16)
        state[u] = (jnp.full((1, tq), -jnp.inf, F32), jnp.zeros((dv + DA_ONES_ROWS, tq), F32))

    def scores(u, chunk):
        return _dot_nt(chunk[u[2]][...], qm[u])

    def update(u, chunk, st):
        hh = u[1]
        vt_aug = jnp.concatenate([chunk[2][hh * dv:(hh + 1) * dv, :], ones], axis=0)
        mx, acc = state[u]
        mx_new = jnp.maximum(mx, jnp.max(st, axis=0, keepdims=True))
        pt = jnp.exp2(st - mx_new).astype(BF16)
        state[u] = (mx_new, jnp.exp2(mx - mx_new) * acc + _dot(vt_aug, pt))

    cur = {u: scores(u, chunks[0]) for u in units}
    for c, chunk in enumerate(chunks):
        nxt = {}
        for u in units:
            if c + 1 < len(chunks):
                nxt[u] = scores(u, chunks[c + 1])
            update(u, chunk, cur[u])
        cur = nxt

    for sub in range(q1_ref.shape[1] // tq):
        rows = slice(sub * tq, (sub + 1) * tq)
        for hh in range(2):
            cols = slice(hh * dv, (hh + 1) * dv)
            a1, a2 = state[sub, hh, 0][1], state[sub, hh, 1][1]
            ot = a1[:dv] * (1.0 / a1[dv:dv + 1]) - a2[:dv] * (lam / a2[dv:dv + 1])
            oft = ot * lax.rsqrt(jnp.mean(ot * ot, axis=0, keepdims=True) + EPS)
            y = oft.T * sg_ref[...] * (1.0 - lam_init) * _silu(g_ref[0, rows, cols].astype(F32))
            o_ref[0, rows, cols] = y.astype(o_ref.dtype)


def _diff_attn(z, vt, blk, d_b, lam_vec, subln_g, layer, lam_init, cache):
    B, L, _ = z.shape
    tq = min(L, DA_TQ)
    bq = min(L, DA_SUBTILES * tq)
    tk = min(L, DA_TK)
    n_pairs = d_b // (4 * HEAD_DIM)
    col128 = lambda name: blk[name] * (d_b // LANES)
    col256 = lambda name: blk[name] * (d_b // (2 * LANES))
    qspec = lambda m: pl.BlockSpec((1, bq, LANES), lambda b, p, i: (b, i, col128("q_b") + m * n_pairs + p))
    kspec = lambda m: pl.BlockSpec((1, L, LANES), lambda b, p, i: (b, 0, col128("k_b") + m * n_pairs + p))
    ck, cv = cache
    P = ck.shape[4]
    assert P % tk == 0 and L % tk == 0 and L % bq == 0
    kcspec = lambda m: pl.BlockSpec((1, 1, 1, 2, P, HEAD_DIM), lambda b, p, i: (b, layer, m, p, 0, 0))
    in_specs = [
        pl.BlockSpec(memory_space=pltpu.SMEM),
        qspec(0), qspec(1), kspec(0), kspec(1),
        pl.BlockSpec((1, 2 * LANES, L), lambda b, p, i: (b, p, 0)),
        pl.BlockSpec((1, bq, 2 * LANES), lambda b, p, i: (b, i, col256("g_b") + p)),
        kcspec(0), kcspec(1),
        pl.BlockSpec((1, 1, 2, P, 2 * HEAD_DIM), lambda b, p, i: (b, layer, p, 0, 0)),
        pl.BlockSpec((1, 2 * HEAD_DIM), lambda b, p, i: (0, 0)),
    ]
    return pl.pallas_call(
        functools.partial(_diff_attn_kernel, layer=layer, lam_init=lam_init, tq=tq, tk=tk, n_self=L // tk,
                          n_ctx=P // tk),
        grid=(B, n_pairs, L // bq),
        in_specs=in_specs,
        out_specs=pl.BlockSpec((1, bq, 2 * LANES), lambda b, p, i: (b, i, p)),
        out_shape=jax.ShapeDtypeStruct((B, L, d_b), BF16),
        compiler_params=_cparams(("parallel", "parallel", "parallel")),
        name="diff_attn_latent",
    )(lam_vec, z, z, z, z, vt, z, ck, ck, cv, subln_g)


def _ctx_attn_kernel(lam_ref, qb_ref, kb_ref, vb_ref, gb_ref, qc_ref, kc_ref, vc_ref, gc_ref, sg_ref,
                     yb_ref, yc_ref, *, layer, lam_init):
    dv = 2 * HEAD_DIM
    L = qb_ref.shape[1]
    lam = lam_ref[layer]
    n_pairs_b = qb_ref.shape[2] // (2 * LANES)
    n_pairs_c = qc_ref.shape[2] // LANES
    ones = jnp.ones((L, LANES), BF16)

    def masked(q, hh):
        return jnp.where(_head_lanes(hh), q, jnp.zeros_like(q)).astype(BF16)

    s_b, s_c = {}, {}
    for p in range(n_pairs_b):
        for m in range(2):
            cols = slice((m * n_pairs_b + p) * LANES, (m * n_pairs_b + p + 1) * LANES)
            k = kb_ref[0, :, cols].astype(BF16)
            for hh in range(2):
                s_b[p, hh, m] = _dot_nt(masked(qb_ref[0, :, cols], hh), k)
    for p in range(n_pairs_c):
        cols = slice(p * LANES, (p + 1) * LANES)
        k = kc_ref[0, :, cols].astype(BF16)
        for hh in range(2):
            s_c[p, hh] = _dot_nt(masked(qc_ref[0, :, cols], hh), k)

    def attend(s, vaug):
        o = _dot(jnp.exp2(s - jnp.max(s, axis=-1, keepdims=True)).astype(BF16), vaug)
        return o[:, :vaug.shape[1] - LANES] * (1.0 / o[:, vaug.shape[1] - LANES:vaug.shape[1] - LANES + 1])

    for p in range(n_pairs_b):
        for hh in range(2):
            cols = slice((2 * p + hh) * dv, (2 * p + hh + 1) * dv)
            vaug = jnp.concatenate([vb_ref[0, :, cols].astype(BF16), ones], axis=1)
            o = attend(s_b[p, hh, 0], vaug) - lam * attend(s_b[p, hh, 1], vaug)
            of = o * lax.rsqrt(jnp.mean(o * o, axis=-1, keepdims=True) + EPS)
            y = of * sg_ref[...] * (1.0 - lam_init) * _silu(gb_ref[0, :, cols].astype(F32))
            yb_ref[0, :, cols] = y.astype(yb_ref.dtype)

    lo = _head_lanes(0)
    for p in range(n_pairs_c):
        cols = slice(p * LANES, (p + 1) * LANES)
        vaug = jnp.concatenate([vc_ref[0, :, cols].astype(BF16), ones], axis=1)
        o = jnp.where(lo, attend(s_c[p, 0], vaug), attend(s_c[p, 1], vaug))
        yc_ref[0, :, cols] = (o * _silu(gc_ref[0, :, cols].astype(F32))).astype(yc_ref.dtype)


def _ctx_attn(z, blk, d_b, d_c, lam_vec, subln_g, layer, lam_init):
    B, L, _ = z.shape
    assert d_b == d_c
    spec = lambda name: pl.BlockSpec((1, L, d_b), lambda b, c=blk[name]: (b, 0, c))
    names = ("q_b", "k_b", "v_b", "g_b", "q_c", "k_c", "v_c", "g_c")
    return pl.pallas_call(
        functools.partial(_ctx_attn_kernel, layer=layer, lam_init=lam_init),
        grid=(B,),
        in_specs=[pl.BlockSpec(memory_space=pltpu.SMEM), *[spec(n) for n in names],
                  pl.BlockSpec((1, 2 * HEAD_DIM), lambda b: (0, 0))],
        out_specs=[pl.BlockSpec((1, L, d_b), lambda b: (b, 0, 0)), pl.BlockSpec((1, L, d_c), lambda b: (b, 0, 0))],
        out_shape=[jax.ShapeDtypeStruct((B, L, d_b), BF16), jax.ShapeDtypeStruct((B, L, d_c), BF16)],
        compiler_params=_cparams(("parallel",)),
        name="ctx_attn",
    )(lam_vec, *([z] * len(names)), subln_g)


def _na_kernel(q_ref, k_ref, v_ref, g_ref, kc_ref, vc_ref, bias_ref, o_ref, *, rows):
    lane = lax.broadcasted_iota(jnp.int32, (1, LANES), 1)
    n_loc = WIN_R * GRID_W
    P = kc_ref.shape[3]
    kc = _pair_lanes(kc_ref.at[0, 0])
    vc_aug = jnp.concatenate([_pair_lanes(vc_ref.at[0, 0]), jnp.ones((P, LANES), BF16)], axis=1)
    ones_loc = jnp.ones((n_loc, LANES), BF16)
    tq = NA_ROWS_PER_BLOCK * GRID_W
    lo = lane < HEAD_DIM

    def block(i, carry):
        q0 = pl.multiple_of(i * tq, tq)
        q = q_ref[0, pl.ds(q0, tq), :]
        zero = jnp.zeros_like(q)
        q_heads = (jnp.where(lo, q, zero), jnp.where(lo, zero, q))
        q2 = jnp.concatenate([qh[j * GRID_W:(j + 1) * GRID_W] for j in range(NA_ROWS_PER_BLOCK) for qh in q_heads],
                             axis=0)
        s_ctx = _dot_nt(q2, kc)
        m_ctx = jnp.max(s_ctx, axis=-1, keepdims=True)
        o_rows, p_ctx_rows = [], []
        for j in range(NA_ROWS_PER_BLOCK):
            r = i * NA_ROWS_PER_BLOCK + j
            rs = jnp.clip(r - WIN_R // 2, 0, rows - WIN_R)
            off = rs - r + WIN_R - 1
            k0 = pl.multiple_of(rs * GRID_W, GRID_W)
            kl = k_ref[0, pl.ds(k0, n_loc), :]
            vl_aug = jnp.concatenate([v_ref[0, pl.ds(k0, n_loc), :], ones_loc], axis=1)
            rsl = slice(2 * j * GRID_W, 2 * (j + 1) * GRID_W)
            s_loc = _dot_nt(q2[rsl], kl) + bias_ref[0, :, off].reshape(2 * GRID_W, n_loc)
            m = jnp.maximum(jnp.max(s_loc, axis=-1, keepdims=True), m_ctx[rsl])
            p_ctx_rows.append(jnp.exp2(s_ctx[rsl] - m).astype(BF16))
            o_rows.append(_dot(jnp.exp2(s_loc - m).astype(BF16), vl_aug))
        o = jnp.concatenate(o_rows, axis=0) + _dot(jnp.concatenate(p_ctx_rows, axis=0), vc_aug)
        o = o[:, :LANES] * (1.0 / o[:, LANES:LANES + 1])
        o = jnp.concatenate(
            [jnp.where(lo, o[2 * j * GRID_W:(2 * j + 1) * GRID_W], o[(2 * j + 1) * GRID_W:(2 * j + 2) * GRID_W])
             for j in range(NA_ROWS_PER_BLOCK)], axis=0)
        y = o * _silu(g_ref[0, pl.ds(q0, tq), :].astype(F32))
        o_ref[0, pl.ds(q0, tq), :] = y.astype(o_ref.dtype)
        return carry

    n_blocks = rows // NA_ROWS_PER_BLOCK
    lax.fori_loop(0, n_blocks, block, 0, unroll=math.gcd(n_blocks, NA_BLOCK_UNROLL))


def _na_attn(z, blk, d_c, nk, nv, bias_tab, layer):
    B, L, _ = z.shape
    rows = L // GRID_W
    assert rows >= WIN_R and L % GRID_W == 0 and rows % NA_ROWS_PER_BLOCK == 0
    P = nk.shape[3]
    n_pairs = d_c // LANES
    spec = lambda name: pl.BlockSpec((1, L, LANES), lambda b, p, c=blk[name] * n_pairs: (b, 0, c + p))
    cspec = pl.BlockSpec((1, 1, 2, P, HEAD_DIM), lambda b, p: (b, layer, p, 0, 0))
    return pl.pallas_call(
        functools.partial(_na_kernel, rows=rows),
        grid=(B, n_pairs),
        in_specs=[
            spec("q_c"), spec("k_c"), spec("v_c"), spec("g_c"), cspec, cspec,
            pl.BlockSpec((1, 2, WIN_R, GRID_W, WIN_R * GRID_W), lambda b, p: (layer, p, 0, 0, 0)),
        ],
        out_specs=pl.BlockSpec((1, L, LANES), lambda b, p: (b, 0, p)),
        out_shape=jax.ShapeDtypeStruct((B, L, d_c), BF16),
        compiler_params=_cparams(("parallel", "parallel")),
        name="na_attn_latent",
    )(z, z, z, z, nk, nv, bias_tab)


def _spatial_gate(u, v, g, ng_ref, nb_ref, ws_ref, bs_ref):
    n_groups = v.shape[1] // SG_GROUP_W
    vn = (_layernorm(v.astype(F32)) * ng_ref[...] + nb_ref[...]).astype(BF16)
    ys = []
    for n in range(v.shape[0] // CHUNK):
        rows = slice(n * CHUNK, (n + 1) * CHUNK)
        sv = jnp.concatenate(
            [_dot(ws_ref[gi], vn[rows, gi * SG_GROUP_W:(gi + 1) * SG_GROUP_W]) for gi in range(n_groups)],
            axis=1) + bs_ref[...]
        ys.append((u[rows].astype(F32) * sv * _silu(g[rows].astype(F32))).astype(BF16))
    return jnp.concatenate(ys, axis=0)


def _merge_kernel(x_ref, h_ref, u_ref, v_ref, g_ref, yb_ref, yc_ref, mod_ref, ng_ref, nb_ref, ws_ref, bs_ref,
                  wmg_ref, bmg_ref, wa_ref, wb_ref, wc_ref, wo_ref, lg_ref, lb_ref, o_ref, *, alpha, ts):
    D = x_ref.shape[2]
    for sub in range(x_ref.shape[1] // ts):
        rows = slice(sub * ts, (sub + 1) * ts)
        h = h_ref[0, rows, :]
        ya = _spatial_gate(u_ref[0, rows, :], v_ref[0, rows, :], g_ref[0, rows, :], ng_ref, nb_ref, ws_ref, bs_ref)
        m = None
        for i, (y, w_ref) in enumerate(((ya, wa_ref), (yb_ref[0, rows, :], wb_ref), (yc_ref[0, rows, :], wc_ref))):
            gate = _sigmoid(_dot(h, wmg_ref[:, i * D:(i + 1) * D]) + bmg_ref[:, i * D:(i + 1) * D])
            t = gate * _dot(y, w_ref[...])
            m = t if m is None else m + t
        out = _dot(m.astype(BF16), wo_ref[...])
        t = alpha * x_ref[0, rows, :] + mod_ref[0, 0, 2:3, :] * out
        o_ref[0, rows, :] = _layernorm(t) * lg_ref[...] + lb_ref[...]


def _merge(x, h, z, blk, yb, yc, mod, mod_row0, layer, sg, wmg, bmg, wa, wb, wc, wo, lg, lb, alpha):
    B, L, D = x.shape
    d_a = wa.shape[1]
    tm = min(L, MERGE_TM)
    ts = tm // MERGE_SUBTILES
    assert ts % CHUNK == 0 and L % tm == 0
    tok = lambda w: pl.BlockSpec((1, tm, w), lambda b, i: (b, i, 0))
    zspec = lambda name: pl.BlockSpec((1, tm, d_a), lambda b, i, c=blk[name]: (b, i, c))
    const = lambda a: pl.BlockSpec((None, *a.shape[1:]), lambda b, i, n=a.ndim - 1: (layer,) + (0,) * n,
                                   pipeline_mode=pl.Buffered(1))
    weights = (*sg, wmg, bmg, wa, wb, wc, wo, lg, lb)
    return pl.pallas_call(
        functools.partial(_merge_kernel, alpha=alpha, ts=ts),
        grid=(B, L // tm),
        in_specs=[
            tok(D), tok(D), zspec("u_a"), zspec("v_a"), zspec("g_a"), tok(yb.shape[2]), tok(yc.shape[2]),
            pl.BlockSpec((1, 1, 3, D), lambda b, i: (mod_row0 + b, 0, 0, 0)),
            *[const(w) for w in weights],
        ],
        out_specs=tok(D),
        out_shape=jax.ShapeDtypeStruct((B, L, D), F32),
        compiler_params=_cparams(("parallel", "parallel")),
        name="merge_postnorm",
    )(x, h, z, z, z, yb, yc, mod, *weights)


def _rope_tables(n_tok):
    half = HEAD_DIM // 2
    quarter = HEAD_DIM // 4
    t = jnp.arange(n_tok)
    inv = 1.0 / (ROPE_THETA ** (jnp.arange(0, half, 2, dtype=F32) / half))
    ang = jnp.stack([(t // GRID_W).astype(F32)[:, None] * inv, (t % GRID_W).astype(F32)[:, None] * inv], axis=1)
    cos = jnp.cos(ang)[:, :, None, :]
    sin = jnp.sin(ang)[:, :, None, :]
    zero = jnp.zeros_like(sin)
    shape = (n_tok, 2, 2, quarter)
    c = jnp.broadcast_to(cos, shape).reshape(n_tok, HEAD_DIM)
    s_up = jnp.concatenate([-sin, zero], axis=2).reshape(n_tok, HEAD_DIM)
    s_dn = jnp.concatenate([zero, sin], axis=2).reshape(n_tok, HEAD_DIM)
    rep = LANES // HEAD_DIM
    return tuple(jnp.tile(a, (1, rep)) for a in (c, s_up, s_dn))


def _na_bias_kernel(rb_ref, o_ref):
    shape = (GRID_W, LANES)
    qcol = lax.broadcasted_iota(jnp.int32, shape, 0)
    lane = lax.broadcasted_iota(jnp.int32, shape, 1)
    kcol = lane & (GRID_W - 1)
    cstart = jnp.clip(qcol - WIN_W // 2, 0, GRID_W - WIN_W)
    in_window = (kcol >= cstart) & (kcol < cstart + WIN_W)

    def toeplitz(y, lane0):
        row = jnp.broadcast_to(rb_ref[0, y:y + 1, :], shape)
        return pltpu.roll(row, (lane0 - (WIN_W - 1)) % LANES, 1, stride=1, stride_axis=0)

    n_dy = 2 * WIN_R - 1
    t_lo = [toeplitz(y, 0) for y in range(n_dy)]
    t_hi = [toeplitz(y, GRID_W) for y in range(n_dy)]
    for off in range(WIN_R):
        for jp in range(WIN_R // 2):
            y = off + 2 * jp
            tile = jnp.where(lane < GRID_W, t_lo[y], t_hi[y + 1]) * LOG2E
            o_ref[0, 0, off, :, jp * LANES:(jp + 1) * LANES] = jnp.where(in_window, tile, NEG_INF)


def _na_bias_tables(rel_bias):
    depth, n_heads, n_dy, n_dx = rel_bias.shape
    assert n_dy == 2 * WIN_R - 1 and n_dx == 2 * WIN_W - 1 and WIN_R % 2 == 0
    rb = jnp.pad(rel_bias.astype(F32), ((0, 0), (0, 0), (0, 16 - n_dy), (0, LANES - n_dx)))
    return pl.pallas_call(
        _na_bias_kernel,
        grid=(depth, n_heads),
        in_specs=[pl.BlockSpec((1, 16, LANES), lambda l, h: (l * n_heads + h, 0, 0))],
        out_specs=pl.BlockSpec((1, 1, WIN_R, GRID_W, WIN_R * GRID_W), lambda l, h: (l, h, 0, 0, 0)),
        out_shape=jax.ShapeDtypeStruct((depth, n_heads, WIN_R, GRID_W, WIN_R * GRID_W), F32),
        compiler_params=_cparams(("parallel", "parallel")),
        name="na_bias_table",
    )(rb.reshape(depth * n_heads, 16, LANES))


def kernel(x_prompt, x_sample, c, cache_diff_k, cache_diff_v, cache_na_k, cache_na_v, c_ctx, w_ada, b_ada, w_in, sg_norm_g, sg_norm_b, w_spatial, b_spatial, lambda_q1, lambda_k1, lambda_q2, lambda_k2, diff_subln_g, na_rel_bias, w_br_a, w_br_b, w_br_c, w_mgate, b_mgate, w_out, ln_g, ln_b):
    depth, D, d_in = w_in.shape
    batch, seq, _ = x_prompt.shape
    dec_batch, dec_seq, _ = x_sample.shape
    d_a, d_b, d_c = w_br_a.shape[1], w_br_b.shape[1], w_br_c.shape[1]
    h_b = d_b // (2 * HEAD_DIM)
    h_c = d_c // HEAD_DIM
    tn = d_a
    assert d_a == d_b == d_c and d_in == 11 * tn and tn % (2 * LANES) == 0
    assert d_a // SG_GROUP_W == w_spatial.shape[1] and w_spatial.shape[2] == CHUNK
    names = ("u_a", "v_a", "g_a", "q_b", "k_b", "v_b", "g_b", "q_c", "k_c", "v_c", "g_c")
    blk = {n: i for i, n in enumerate(names)}
    alpha = (2 * depth) ** 0.25
    lam_inits = tuple(0.8 - 0.6 * math.exp(-0.3 * l) for l in range(depth))

    mod_rows = -(-(1 + dec_batch) // 8) * 8
    cvec = jnp.zeros((mod_rows, D), F32).at[0].set(c_ctx).at[1:1 + dec_batch].set(c)
    mod = _ada_mod(cvec, w_ada, b_ada).reshape(depth * mod_rows, 1, 3, D)
    lam_vec = _diff_lambdas(lambda_q1, lambda_k1, lambda_q2, lambda_k2, lam_inits)

    w_in_b = w_in.astype(BF16)
    w_s_b = w_spatial.astype(BF16)
    b_s_full = jnp.repeat(jnp.swapaxes(b_spatial, 1, 2), SG_GROUP_W, axis=2)
    wmg_b, wa_b, wb_b, wc_b, wo_b = (w.astype(BF16) for w in (w_mgate, w_br_a, w_br_b, w_br_c, w_out))
    rope_tabs = _rope_tables(dec_seq)
    bias_tab = _na_bias_tables(na_rel_bias)

    xp = x_prompt.reshape(1, batch * seq, D)
    xs = x_sample
    new_cache = (jnp.zeros((batch, depth, 2, h_b, seq, HEAD_DIM), F32),
                 jnp.zeros((batch, depth, h_b, seq, 2 * HEAD_DIM), F32),
                 jnp.zeros((batch, depth, h_c, seq, HEAD_DIM), F32),
                 jnp.zeros((batch, depth, h_c, seq, HEAD_DIM), F32))
    rowv = lambda a: a.reshape(depth, 1, -1)
    merge_w = (wmg_b, rowv(b_mgate), wa_b, wb_b, wc_b, wo_b, rowv(ln_g), rowv(ln_b))
    sg = (rowv(sg_norm_g), rowv(sg_norm_b), w_s_b, b_s_full)
    for l in range(depth):
        row2 = lambda a: a[l].reshape(1, -1)

        h, z, *new_cache = _inproj(xp, mod, l * mod_rows, w_in_b, blk, tn, new_cache=new_cache, layer=l)
        zc = z.reshape(batch, seq, d_in)
        yb, yc = _ctx_attn(zc, blk, d_b, d_c, lam_vec, row2(diff_subln_g), l, lam_inits[l])
        flat = lambda a: a.reshape(1, batch * seq, a.shape[-1])
        xp = _merge(xp, h, z, blk, flat(yb), flat(yc), mod, l * mod_rows, l, sg, *merge_w, alpha)

        h, z, vt = _inproj(xs, mod, l * mod_rows + 1, w_in_b, blk, tn, rope_tabs=rope_tabs, layer=l)
        yb = _diff_attn(z, vt, blk, d_b, lam_vec, row2(diff_subln_g), l, lam_inits[l],
                        cache=(cache_diff_k, cache_diff_v))
        yc = _na_attn(z, blk, d_c, cache_na_k, cache_na_v, bias_tab, l)
        xs = _merge(xs, h, z, blk, yb, yc, mod, l * mod_rows + 1, l, sg, *merge_w, alpha)

    return (xp.reshape(batch, seq, D), xs, *new_cache)
```

```python
import functools
import math

import jax
import jax.numpy as jnp
from jax import lax
from jax.experimental import pallas as pl
from jax.experimental.pallas import tpu as pltpu

GRID_W = 64
CHUNK = 128
HEAD_DIM = 64
SG_GROUP_W = 128
WIN_R = 8
WIN_W = 16
DA_TQ = 256
DA_TK = 256
DA_SUBTILES = 2
DA_ONES_ROWS = 16
INPROJ_SUBTILES = 2
MERGE_TM = 512
MERGE_SUBTILES = 2
NA_ROWS_PER_BLOCK = 4
NA_BLOCK_UNROLL = 16
ROPE_THETA = 10000.0
EPS = 1e-6
NEG_INF = -1e30
LANES = 128
LOG2E = math.log2(math.e)
Q_SCALE = HEAD_DIM ** -0.5 * LOG2E

BF16 = jnp.bfloat16
F32 = jnp.float32

VMEM_LIMIT = 56 * 1024 * 1024


def _cparams(sem):
    return pltpu.CompilerParams(dimension_semantics=sem, vmem_limit_bytes=VMEM_LIMIT)


def _sigmoid(x):
    return 1.0 / (1.0 + jnp.exp(-x))


def _silu(x):
    return x * _sigmoid(x)


def _dot(a, b):
    return jnp.dot(a, b, preferred_element_type=F32)


def _dot_nt(a, b):
    return lax.dot_general(a, b, (((1,), (1,)), ((), ())), preferred_element_type=F32)


def _layernorm(x):
    mu = jnp.mean(x, axis=-1, keepdims=True)
    xc = x - mu
    var = jnp.mean(xc * xc, axis=-1, keepdims=True)
    return xc * lax.rsqrt(var + EPS)


def _ada_kernel(c_ref, w_ref, b_ref, o_ref):
    s = _silu(c_ref[...]).astype(BF16)
    o_ref[0] = _dot(s, w_ref[0].astype(BF16)) + b_ref[0]


def _ada_mod(cvec, w_ada, b_ada):
    depth, d, d3 = w_ada.shape
    rows = cvec.shape[0]
    tn = 1024
    return pl.pallas_call(
        _ada_kernel,
        grid=(depth, d3 // tn),
        in_specs=[
            pl.BlockSpec((rows, d), lambda l, j: (0, 0)),
            pl.BlockSpec((1, d, tn), lambda l, j: (l, 0, j)),
            pl.BlockSpec((1, 1, tn), lambda l, j: (l, 0, j)),
        ],
        out_specs=pl.BlockSpec((1, rows, tn), lambda l, j: (l, 0, j)),
        out_shape=jax.ShapeDtypeStruct((depth, rows, d3), F32),
        compiler_params=_cparams(("parallel", "parallel")),
        name="ada_mod",
    )(cvec, w_ada, b_ada.reshape(depth, 1, d3))


def _lambda_kernel(lam_ref, o_ref, *, lam_inits):
    p = lam_ref[...]
    t1 = jnp.sum(p[:, 0, :] * p[:, 1, :], axis=-1, keepdims=True)
    t2 = jnp.sum(p[:, 2, :] * p[:, 3, :], axis=-1, keepdims=True)
    layer = lax.broadcasted_iota(jnp.int32, t1.shape, 0)
    init = jnp.zeros_like(t1)
    for l, v in enumerate(lam_inits):
        init = jnp.where(layer == l, v, init)
    o_ref[...] = jnp.broadcast_to(jnp.exp(t1) - jnp.exp(t2) + init, o_ref.shape)


def _diff_lambdas(lq1, lk1, lq2, lk2, lam_inits):
    depth = lq1.shape[0]
    params = jnp.stack([lq1, lk1, lq2, lk2], axis=1).astype(F32)
    out = pl.pallas_call(
        functools.partial(_lambda_kernel, lam_inits=lam_inits),
        out_shape=jax.ShapeDtypeStruct((depth, LANES), F32),
        name="diff_lambda",
    )(params)
    return out[:, 0]


def _rope(a, c, s_up, s_dn):
    outs = []
    for i in range(a.shape[1] // LANES):
        g = a[:, i * LANES:(i + 1) * LANES]
        outs.append(g * c + pltpu.roll(g, LANES - 16, 1) * s_up + pltpu.roll(g, 16, 1) * s_dn)
    return jnp.concatenate(outs, axis=1)


def _store_heads(out_ref, lead, rows, acc):
    width = out_ref.shape[-1]
    for hd in range(out_ref.shape[-3]):
        out_ref[(*lead, hd, rows, slice(None))] = acc[:, hd * width:(hd + 1) * width]


def _inproj_kernel(*refs, rope, blk, tn, ts):
    if rope:
        x_ref, mod_ref, w_ref, c_ref, su_ref, sd_ref, h_ref, z_ref, vt_ref = refs
    else:
        x_ref, mod_ref, w_ref, _, _, _, _, h_ref, z_ref, dk_ref, dv_ref, nk_ref, nv_ref = refs
    for sub in range(x_ref.shape[1] // ts):
        rows = slice(sub * ts, (sub + 1) * ts)
        xn = _layernorm(x_ref[0, rows, :])
        hb = (xn * (1.0 + mod_ref[0, 0, 1:2, :]) + mod_ref[0, 0, 0:1, :]).astype(BF16)
        h_ref[0, rows, :] = hb
        for j in range(w_ref.shape[1] // tn):
            cols = slice(j * tn, (j + 1) * tn)
            acc = _dot(hb, w_ref[:, cols])
            if rope and j in (blk["q_b"], blk["k_b"]):
                acc = _rope(acc, c_ref[rows, :], su_ref[rows, :], sd_ref[rows, :])
            if j in (blk["q_b"], blk["q_c"]):
                acc = acc * Q_SCALE
            z_ref[0, rows, cols] = acc.astype(z_ref.dtype)
            if rope and j == blk["v_b"]:
                vt_ref[0, :, rows] = acc.T.astype(vt_ref.dtype)
            if not rope:
                if j == blk["k_b"]:
                    for m in range(dk_ref.shape[2]):
                        half = dk_ref.shape[3] * dk_ref.shape[5]
                        _store_heads(dk_ref, (0, 0, m), rows, acc[:, m * half:(m + 1) * half])
                elif j == blk["v_b"]:
                    _store_heads(dv_ref, (0, 0), rows, acc)
                elif j == blk["k_c"]:
                    _store_heads(nk_ref, (0, 0), rows, acc)
                elif j == blk["v_c"]:
                    _store_heads(nv_ref, (0, 0), rows, acc)


def _inproj(x, mod, mod_row0, w, blk, tn, rope_tabs=None, new_cache=None, layer=None):
    B, L, D = x.shape
    d_in = w.shape[2]
    rope = rope_tabs is not None
    tm = min(L, 512) if rope else new_cache[0].shape[-2]
    in_specs = [
        pl.BlockSpec((1, tm, D), lambda b, i: (b, i, 0)),
        pl.BlockSpec((1, 1, 3, D), lambda b, i: (mod_row0 + b, 0, 0, 0)),
        pl.BlockSpec((None, D, d_in), lambda b, i: (layer, 0, 0), pipeline_mode=pl.Buffered(1)),
    ]
    args = [x, mod, w]
    out_specs = [
        pl.BlockSpec((1, tm, D), lambda b, i: (b, i, 0)),
        pl.BlockSpec((1, tm, d_in), lambda b, i: (b, i, 0)),
    ]
    out_shape = [
        jax.ShapeDtypeStruct((B, L, D), BF16),
        jax.ShapeDtypeStruct((B, L, d_in), BF16 if rope else F32),
    ]
    aliases = {}
    if rope:
        in_specs += [pl.BlockSpec((tm, LANES), lambda b, i: (i, 0))] * 3
        args += list(rope_tabs)
        out_specs.append(pl.BlockSpec((1, tn, tm), lambda b, i: (b, 0, i)))
        out_shape.append(jax.ShapeDtypeStruct((B, tn, L), BF16))
    else:
        assert B == 1 and L == new_cache[0].shape[0] * tm and tm % INPROJ_SUBTILES == 0
        for a in new_cache:
            aliases[len(args)] = len(out_shape)
            in_specs.append(pl.BlockSpec(memory_space=pl.ANY))
            args.append(a)
            tail = a.shape[2:]
            out_specs.append(pl.BlockSpec((1, 1, *tail), lambda b, i, n=len(tail): (i, layer) + (0,) * n))
            out_shape.append(jax.ShapeDtypeStruct(a.shape, a.dtype))
    return pl.pallas_call(
        functools.partial(_inproj_kernel, rope=rope, blk=blk, tn=tn, ts=tm // INPROJ_SUBTILES),
        grid=(B, L // tm),
        in_specs=in_specs,
        out_specs=out_specs,
        out_shape=out_shape,
        input_output_aliases=aliases,
        compiler_params=_cparams(("parallel", "parallel")),
        name="inproj_rope" if rope else "inproj",
    )(*args)


def _head_lanes(hh):
    lane = lax.broadcasted_iota(jnp.int32, (1, LANES), 1)
    return (lane < HEAD_DIM) if hh == 0 else (lane >= HEAD_DIM)


def _pair_lanes(ref):
    return jnp.concatenate([ref[0], ref[1]], axis=-1).astype(BF16)


def _diff_attn_kernel(lam_ref, q1_ref, q2_ref, k1_ref, k2_ref, vt_ref, g_ref, kc1_ref, kc2_ref, vc_ref, sg_ref,
                      o_ref, *, layer, lam_init, tq, tk, n_self, n_ctx):
    lam = lam_ref[layer]
    dv = 2 * HEAD_DIM
    ones = jnp.ones((DA_ONES_ROWS, tk), BF16)
    kc = [_pair_lanes(r.at[0, 0, 0]) for r in (kc1_ref, kc2_ref)]
    vct = jnp.concatenate([vc_ref[0, 0, hd].T for hd in range(2)], axis=0).astype(BF16)
    chunks = [(k1_ref.at[0, c * tk:(c + 1) * tk, :], k2_ref.at[0, c * tk:(c + 1) * tk, :],
               vt_ref.at[0, :, c * tk:(c + 1) * tk]) for c in range(n_self)]
    chunks += [(kc[0][c * tk:(c + 1) * tk], kc[1][c * tk:(c + 1) * tk], vct[:, c * tk:(c + 1) * tk])
               for c in range(n_ctx)]
    pairs = [(sub, m) for sub in range(q1_ref.shape[1] // tq) for m in range(2)]
    qm, state = {}, {}
    for sub, m in pairs:
        q = (q1_ref, q2_ref)[m][0, sub * tq:(sub + 1) * tq, :]
        qm[sub, m] = jnp.concatenate([jnp.where(_head_lanes(hh), q, jnp.zeros_like(q)) for hh in range(2)],
                                     axis=0).astype(BF16)
        for hh in range(2):
            state[sub, hh, m] = (jnp.full((1, tq), -jnp.inf, F32), jnp.zeros((dv + DA_ONES_ROWS, tq), F32))

    def scores(pair, chunk):
        return _dot_nt(chunk[pair[1]][...], qm[pair])

    def update(u, chunk, st):
        hh = u[1]
        vt_aug = jnp.concatenate([chunk[2][hh * dv:(hh + 1) * dv, :], ones], axis=0)
        mx, acc = state[u]
        mx_new = jnp.maximum(mx, jnp.max(st, axis=0, keepdims=True))
        pt = jnp.exp2(st - mx_new).astype(BF16)
        state[u] = (mx_new, jnp.exp2(mx - mx_new) * acc + _dot(vt_aug, pt))

    cur = {p: scores(p, chunks[0]) for p in pairs}
    for c, chunk in enumerate(chunks):
        nxt = {}
        for p in pairs:
            if c + 1 < len(chunks):
                nxt[p] = scores(p, chunks[c + 1])
            for hh in range(2):
                update((p[0], hh, p[1]), chunk, cur[p][:, hh * tq:(hh + 1) * tq])
        cur = nxt

    for sub in range(q1_ref.shape[1] // tq):
        rows = slice(sub * tq, (sub + 1) * tq)
        for hh in range(2):
            cols = slice(hh * dv, (hh + 1) * dv)
            a1, a2 = state[sub, hh, 0][1], state[sub, hh, 1][1]
            ot = a1[:dv] * (1.0 / a1[dv:dv + 1]) - a2[:dv] * (lam / a2[dv:dv + 1])
            oft = ot * lax.rsqrt(jnp.mean(ot * ot, axis=0, keepdims=True) + EPS)
            y = oft.T * sg_ref[...] * (1.0 - lam_init) * _silu(g_ref[0, rows, cols].astype(F32))
            o_ref[0, rows, cols] = y.astype(o_ref.dtype)


def _diff_attn(z, vt, blk, d_b, lam_vec, subln_g, layer, lam_init, cache):
    B, L, _ = z.shape
    tq = min(L, DA_TQ)
    bq = min(L, DA_SUBTILES * tq)
    tk = min(L, DA_TK)
    n_pairs = d_b // (4 * HEAD_DIM)
    col128 = lambda name: blk[name] * (d_b // LANES)
    col256 = lambda name: blk[name] * (d_b // (2 * LANES))
    qspec = lambda m: pl.BlockSpec((1, bq, LANES), lambda b, p, i: (b, i, col128("q_b") + m * n_pairs + p))
    kspec = lambda m: pl.BlockSpec((1, L, LANES), lambda b, p, i: (b, 0, col128("k_b") + m * n_pairs + p))
    ck, cv = cache
    P = ck.shape[4]
    assert P % tk == 0 and L % tk == 0 and L % bq == 0
    kcspec = lambda m: pl.BlockSpec((1, 1, 1, 2, P, HEAD_DIM), lambda b, p, i: (b, layer, m, p, 0, 0))
    in_specs = [
        pl.BlockSpec(memory_space=pltpu.SMEM),
        qspec(0), qspec(1), kspec(0), kspec(1),
        pl.BlockSpec((1, 2 * LANES, L), lambda b, p, i: (b, p, 0)),
        pl.BlockSpec((1, bq, 2 * LANES), lambda b, p, i: (b, i, col256("g_b") + p)),
        kcspec(0), kcspec(1),
        pl.BlockSpec((1, 1, 2, P, 2 * HEAD_DIM), lambda b, p, i: (b, layer, p, 0, 0)),
        pl.BlockSpec((1, 2 * HEAD_DIM), lambda b, p, i: (0, 0)),
    ]
    return pl.pallas_call(
        functools.partial(_diff_attn_kernel, layer=layer, lam_init=lam_init, tq=tq, tk=tk, n_self=L // tk,
                          n_ctx=P // tk),
        grid=(B, n_pairs, L // bq),
        in_specs=in_specs,
        out_specs=pl.BlockSpec((1, bq, 2 * LANES), lambda b, p, i: (b, i, p)),
        out_shape=jax.ShapeDtypeStruct((B, L, d_b), BF16),
        compiler_params=_cparams(("parallel", "parallel", "parallel")),
        name="diff_attn_latent",
    )(lam_vec, z, z, z, z, vt, z, ck, ck, cv, subln_g)


def _ctx_attn_kernel(lam_ref, qb_ref, kb_ref, vb_ref, gb_ref, qc_ref, kc_ref, vc_ref, gc_ref, sg_ref,
                     yb_ref, yc_ref, *, layer, lam_init):
    dv = 2 * HEAD_DIM
    L = qb_ref.shape[1]
    lam = lam_ref[layer]
    n_pairs_b = qb_ref.shape[2] // (2 * LANES)
    n_pairs_c = qc_ref.shape[2] // LANES
    ones = jnp.ones((L, LANES), BF16)

    def masked(q, hh):
        return jnp.where(_head_lanes(hh), q, jnp.zeros_like(q)).astype(BF16)

    s_b, s_c = {}, {}
    for p in range(n_pairs_b):
        for m in range(2):
            cols = slice((m * n_pairs_b + p) * LANES, (m * n_pairs_b + p + 1) * LANES)
            k = kb_ref[0, :, cols].astype(BF16)
            for hh in range(2):
                s_b[p, hh, m] = _dot_nt(masked(qb_ref[0, :, cols], hh), k)
    for p in range(n_pairs_c):
        cols = slice(p * LANES, (p + 1) * LANES)
        k = kc_ref[0, :, cols].astype(BF16)
        for hh in range(2):
            s_c[p, hh] = _dot_nt(masked(qc_ref[0, :, cols], hh), k)

    def attend(s, vaug):
        o = _dot(jnp.exp2(s - jnp.max(s, axis=-1, keepdims=True)).astype(BF16), vaug)
        return o[:, :vaug.shape[1] - LANES] * (1.0 / o[:, vaug.shape[1] - LANES:vaug.shape[1] - LANES + 1])

    for p in range(n_pairs_b):
        for hh in range(2):
            cols = slice((2 * p + hh) * dv, (2 * p + hh + 1) * dv)
            vaug = jnp.concatenate([vb_ref[0, :, cols].astype(BF16), ones], axis=1)
            o = attend(s_b[p, hh, 0], vaug) - lam * attend(s_b[p, hh, 1], vaug)
            of = o * lax.rsqrt(jnp.mean(o * o, axis=-1, keepdims=True) + EPS)
            y = of * sg_ref[...] * (1.0 - lam_init) * _silu(gb_ref[0, :, cols].astype(F32))
            yb_ref[0, :, cols] = y.astype(yb_ref.dtype)

    lo = _head_lanes(0)
    for p in range(n_pairs_c):
        cols = slice(p * LANES, (p + 1) * LANES)
        vaug = jnp.concatenate([vc_ref[0, :, cols].astype(BF16), ones], axis=1)
        o = jnp.where(lo, attend(s_c[p, 0], vaug), attend(s_c[p, 1], vaug))
        yc_ref[0, :, cols] = (o * _silu(gc_ref[0, :, cols].astype(F32))).astype(yc_ref.dtype)


def _ctx_attn(z, blk, d_b, d_c, lam_vec, subln_g, layer, lam_init):
    B, L, _ = z.shape
    assert d_b == d_c
    spec = lambda name: pl.BlockSpec((1, L, d_b), lambda b, c=blk[name]: (b, 0, c))
    names = ("q_b", "k_b", "v_b", "g_b", "q_c", "k_c", "v_c", "g_c")
    return pl.pallas_call(
        functools.partial(_ctx_attn_kernel, layer=layer, lam_init=lam_init),
        grid=(B,),
        in_specs=[pl.BlockSpec(memory_space=pltpu.SMEM), *[spec(n) for n in names],
                  pl.BlockSpec((1, 2 * HEAD_DIM), lambda b: (0, 0))],
        out_specs=[pl.BlockSpec((1, L, d_b), lambda b: (b, 0, 0)), pl.BlockSpec((1, L, d_c), lambda b: (b, 0, 0))],
        out_shape=[jax.ShapeDtypeStruct((B, L, d_b), BF16), jax.ShapeDtypeStruct((B, L, d_c), BF16)],
        compiler_params=_cparams(("parallel",)),
        name="ctx_attn",
    )(lam_vec, *([z] * len(names)), subln_g)


def _na_kernel(q_ref, k_ref, v_ref, g_ref, kc_ref, vc_ref, bias_ref, o_ref, *, rows):
    lane = lax.broadcasted_iota(jnp.int32, (1, LANES), 1)
    n_loc = WIN_R * GRID_W
    P = kc_ref.shape[3]
    kc = _pair_lanes(kc_ref.at[0, 0])
    vc_aug = jnp.concatenate([_pair_lanes(vc_ref.at[0, 0]), jnp.ones((P, LANES), BF16)], axis=1)
    ones_loc = jnp.ones((n_loc, LANES), BF16)
    tq = NA_ROWS_PER_BLOCK * GRID_W
    lo = lane < HEAD_DIM

    def block(i, carry):
        q0 = pl.multiple_of(i * tq, tq)
        q = q_ref[0, pl.ds(q0, tq), :]
        zero = jnp.zeros_like(q)
        q_heads = (jnp.where(lo, q, zero), jnp.where(lo, zero, q))
        q2 = jnp.concatenate([qh[j * GRID_W:(j + 1) * GRID_W] for j in range(NA_ROWS_PER_BLOCK) for qh in q_heads],
                             axis=0)
        s_ctx = _dot_nt(q2, kc)
        m_ctx = jnp.max(s_ctx, axis=-1, keepdims=True)
        o_rows, p_ctx_rows = [], []
        for j in range(NA_ROWS_PER_BLOCK):
            r = i * NA_ROWS_PER_BLOCK + j
            rs = jnp.clip(r - WIN_R // 2, 0, rows - WIN_R)
            off = rs - r + WIN_R - 1
            k0 = pl.multiple_of(rs * GRID_W, GRID_W)
            kl = k_ref[0, pl.ds(k0, n_loc), :]
            vl_aug = jnp.concatenate([v_ref[0, pl.ds(k0, n_loc), :], ones_loc], axis=1)
            rsl = slice(2 * j * GRID_W, 2 * (j + 1) * GRID_W)
            s_loc = _dot_nt(q2[rsl], kl) + bias_ref[0, :, off].reshape(2 * GRID_W, n_loc)
            m = jnp.maximum(jnp.max(s_loc, axis=-1, keepdims=True), m_ctx[rsl])
            p_ctx_rows.append(jnp.exp2(s_ctx[rsl] - m).astype(BF16))
            o_rows.append(_dot(jnp.exp2(s_loc - m).astype(BF16), vl_aug))
        o = jnp.concatenate(o_rows, axis=0) + _dot(jnp.concatenate(p_ctx_rows, axis=0), vc_aug)
        o = o[:, :LANES] * (1.0 / o[:, LANES:LANES + 1])
        o = jnp.concatenate(
            [jnp.where(lo, o[2 * j * GRID_W:(2 * j + 1) * GRID_W], o[(2 * j + 1) * GRID_W:(2 * j + 2) * GRID_W])
             for j in range(NA_ROWS_PER_BLOCK)], axis=0)
        y = o * _silu(g_ref[0, pl.ds(q0, tq), :].astype(F32))
        o_ref[0, pl.ds(q0, tq), :] = y.astype(o_ref.dtype)
        return carry

    n_blocks = rows // NA_ROWS_PER_BLOCK
    lax.fori_loop(0, n_blocks, block, 0, unroll=math.gcd(n_blocks, NA_BLOCK_UNROLL))


def _na_attn(z, blk, d_c, nk, nv, bias_tab, layer):
    B, L, _ = z.shape
    rows = L // GRID_W
    assert rows >= WIN_R and L % GRID_W == 0 and rows % NA_ROWS_PER_BLOCK == 0
    P = nk.shape[3]
    n_pairs = d_c // LANES
    spec = lambda name: pl.BlockSpec((1, L, LANES), lambda b, p, c=blk[name] * n_pairs: (b, 0, c + p))
    cspec = pl.BlockSpec((1, 1, 2, P, HEAD_DIM), lambda b, p: (b, layer, p, 0, 0))
    return pl.pallas_call(
        functools.partial(_na_kernel, rows=rows),
        grid=(B, n_pairs),
        in_specs=[
            spec("q_c"), spec("k_c"), spec("v_c"), spec("g_c"), cspec, cspec,
            pl.BlockSpec((1, 2, WIN_R, GRID_W, WIN_R * GRID_W), lambda b, p: (layer, p, 0, 0, 0)),
        ],
        out_specs=pl.BlockSpec((1, L, LANES), lambda b, p: (b, 0, p)),
        out_shape=jax.ShapeDtypeStruct((B, L, d_c), BF16),
        compiler_params=_cparams(("parallel", "parallel")),
        name="na_attn_latent",
    )(z, z, z, z, nk, nv, bias_tab)


def _spatial_gate(u, v, g, ng_ref, nb_ref, ws_ref, bs_ref):
    n_groups = v.shape[1] // SG_GROUP_W
    vn = (_layernorm(v.astype(F32)) * ng_ref[...] + nb_ref[...]).astype(BF16)
    ys = []
    for n in range(v.shape[0] // CHUNK):
        rows = slice(n * CHUNK, (n + 1) * CHUNK)
        sv = jnp.concatenate(
            [_dot(ws_ref[gi], vn[rows, gi * SG_GROUP_W:(gi + 1) * SG_GROUP_W]) for gi in range(n_groups)],
            axis=1) + bs_ref[...]
        ys.append((u[rows].astype(F32) * sv * _silu(g[rows].astype(F32))).astype(BF16))
    return jnp.concatenate(ys, axis=0)


def _merge_kernel(x_ref, h_ref, u_ref, v_ref, g_ref, yb_ref, yc_ref, mod_ref, ng_ref, nb_ref, ws_ref, bs_ref,
                  wmg_ref, bmg_ref, wa_ref, wb_ref, wc_ref, wo_ref, lg_ref, lb_ref, o_ref, *, alpha, ts):
    D = x_ref.shape[2]
    for sub in range(x_ref.shape[1] // ts):
        rows = slice(sub * ts, (sub + 1) * ts)
        h = h_ref[0, rows, :]
        ya = _spatial_gate(u_ref[0, rows, :], v_ref[0, rows, :], g_ref[0, rows, :], ng_ref, nb_ref, ws_ref, bs_ref)
        m = None
        for i, (y, w_ref) in enumerate(((ya, wa_ref), (yb_ref[0, rows, :], wb_ref), (yc_ref[0, rows, :], wc_ref))):
            gate = _sigmoid(_dot(h, wmg_ref[:, i * D:(i + 1) * D]) + bmg_ref[:, i * D:(i + 1) * D])
            t = gate * _dot(y, w_ref[...])
            m = t if m is None else m + t
        out = _dot(m.astype(BF16), wo_ref[...])
        t = alpha * x_ref[0, rows, :] + mod_ref[0, 0, 2:3, :] * out
        o_ref[0, rows, :] = _layernorm(t) * lg_ref[...] + lb_ref[...]


def _merge(x, h, z, blk, yb, yc, mod, mod_row0, layer, sg, wmg, bmg, wa, wb, wc, wo, lg, lb, alpha):
    B, L, D = x.shape
    d_a = wa.shape[1]
    tm = min(L, MERGE_TM)
    ts = tm // MERGE_SUBTILES
    assert ts % CHUNK == 0 and L % tm == 0
    tok = lambda w: pl.BlockSpec((1, tm, w), lambda b, i: (b, i, 0))
    zspec = lambda name: pl.BlockSpec((1, tm, d_a), lambda b, i, c=blk[name]: (b, i, c))
    const = lambda a: pl.BlockSpec((None, *a.shape[1:]), lambda b, i, n=a.ndim - 1: (layer,) + (0,) * n,
                                   pipeline_mode=pl.Buffered(1))
    weights = (*sg, wmg, bmg, wa, wb, wc, wo, lg, lb)
    return pl.pallas_call(
        functools.partial(_merge_kernel, alpha=alpha, ts=ts),
        grid=(B, L // tm),
        in_specs=[
            tok(D), tok(D), zspec("u_a"), zspec("v_a"), zspec("g_a"), tok(yb.shape[2]), tok(yc.shape[2]),
            pl.BlockSpec((1, 1, 3, D), lambda b, i: (mod_row0 + b, 0, 0, 0)),
            *[const(w) for w in weights],
        ],
        out_specs=tok(D),
        out_shape=jax.ShapeDtypeStruct((B, L, D), F32),
        compiler_params=_cparams(("parallel", "parallel")),
        name="merge_postnorm",
    )(x, h, z, z, z, yb, yc, mod, *weights)


def _rope_tables(n_tok):
    half = HEAD_DIM // 2
    quarter = HEAD_DIM // 4
    t = jnp.arange(n_tok)
    inv = 1.0 / (ROPE_THETA ** (jnp.arange(0, half, 2, dtype=F32) / half))
    ang = jnp.stack([(t // GRID_W).astype(F32)[:, None] * inv, (t % GRID_W).astype(F32)[:, None] * inv], axis=1)
    cos = jnp.cos(ang)[:, :, None, :]
    sin = jnp.sin(ang)[:, :, None, :]
    zero = jnp.zeros_like(sin)
    shape = (n_tok, 2, 2, quarter)
    c = jnp.broadcast_to(cos, shape).reshape(n_tok, HEAD_DIM)
    s_up = jnp.concatenate([-sin, zero], axis=2).reshape(n_tok, HEAD_DIM)
    s_dn = jnp.concatenate([zero, sin], axis=2).reshape(n_tok, HEAD_DIM)
    rep = LANES // HEAD_DIM
    return tuple(jnp.tile(a, (1, rep)) for a in (c, s_up, s_dn))


def _na_bias_kernel(rb_ref, o_ref):
    shape = (GRID_W, LANES)
    qcol = lax.broadcasted_iota(jnp.int32, shape, 0)
    lane = lax.broadcasted_iota(jnp.int32, shape, 1)
    kcol = lane & (GRID_W - 1)
    cstart = jnp.clip(qcol - WIN_W // 2, 0, GRID_W - WIN_W)
    in_window = (kcol >= cstart) & (kcol < cstart + WIN_W)

    def toeplitz(y, lane0):
        row = jnp.broadcast_to(rb_ref[0, y:y + 1, :], shape)
        return pltpu.roll(row, (lane0 - (WIN_W - 1)) % LANES, 1, stride=1, stride_axis=0)

    n_dy = 2 * WIN_R - 1
    t_lo = [toeplitz(y, 0) for y in range(n_dy)]
    t_hi = [toeplitz(y, GRID_W) for y in range(n_dy)]
    for off in range(WIN_R):
        for jp in range(WIN_R // 2):
            y = off + 2 * jp
            tile = jnp.where(lane < GRID_W, t_lo[y], t_hi[y + 1]) * LOG2E
            o_ref[0, 0, off, :, jp * LANES:(jp + 1) * LANES] = jnp.where(in_window, tile, NEG_INF)


def _na_bias_tables(rel_bias):
    depth, n_heads, n_dy, n_dx = rel_bias.shape
    assert n_dy == 2 * WIN_R - 1 and n_dx == 2 * WIN_W - 1 and WIN_R % 2 == 0
    rb = jnp.pad(rel_bias.astype(F32), ((0, 0), (0, 0), (0, 16 - n_dy), (0, LANES - n_dx)))
    return pl.pallas_call(
        _na_bias_kernel,
        grid=(depth, n_heads),
        in_specs=[pl.BlockSpec((1, 16, LANES), lambda l, h: (l * n_heads + h, 0, 0))],
        out_specs=pl.BlockSpec((1, 1, WIN_R, GRID_W, WIN_R * GRID_W), lambda l, h: (l, h, 0, 0, 0)),
        out_shape=jax.ShapeDtypeStruct((depth, n_heads, WIN_R, GRID_W, WIN_R * GRID_W), F32),
        compiler_params=_cparams(("parallel", "parallel")),
        name="na_bias_table",
    )(rb.reshape(depth * n_heads, 16, LANES))


def kernel(x_prompt, x_sample, c, cache_diff_k, cache_diff_v, cache_na_k, cache_na_v, c_ctx, w_ada, b_ada, w_in, sg_norm_g, sg_norm_b, w_spatial, b_spatial, lambda_q1, lambda_k1, lambda_q2, lambda_k2, diff_subln_g, na_rel_bias, w_br_a, w_br_b, w_br_c, w_mgate, b_mgate, w_out, ln_g, ln_b):
    depth, D, d_in = w_in.shape
    batch, seq, _ = x_prompt.shape
    dec_batch, dec_seq, _ = x_sample.shape
    d_a, d_b, d_c = w_br_a.shape[1], w_br_b.shape[1], w_br_c.shape[1]
    h_b = d_b // (2 * HEAD_DIM)
    h_c = d_c // HEAD_DIM
    tn = d_a
    assert d_a == d_b == d_c and d_in == 11 * tn and tn % (2 * LANES) == 0
    assert d_a // SG_GROUP_W == w_spatial.shape[1] and w_spatial.shape[2] == CHUNK
    names = ("u_a", "v_a", "g_a", "q_b", "k_b", "v_b", "g_b", "q_c", "k_c", "v_c", "g_c")
    blk = {n: i for i, n in enumerate(names)}
    alpha = (2 * depth) ** 0.25
    lam_inits = tuple(0.8 - 0.6 * math.exp(-0.3 * l) for l in range(depth))

    mod_rows = -(-(1 + dec_batch) // 8) * 8
    cvec = jnp.zeros((mod_rows, D), F32).at[0].set(c_ctx).at[1:1 + dec_batch].set(c)
    mod = _ada_mod(cvec, w_ada, b_ada).reshape(depth * mod_rows, 1, 3, D)
    lam_vec = _diff_lambdas(lambda_q1, lambda_k1, lambda_q2, lambda_k2, lam_inits)

    w_in_b = w_in.astype(BF16)
    w_s_b = w_spatial.astype(BF16)
    b_s_full = jnp.repeat(jnp.swapaxes(b_spatial, 1, 2), SG_GROUP_W, axis=2)
    wmg_b, wa_b, wb_b, wc_b, wo_b = (w.astype(BF16) for w in (w_mgate, w_br_a, w_br_b, w_br_c, w_out))
    rope_tabs = _rope_tables(dec_seq)
    bias_tab = _na_bias_tables(na_rel_bias)

    xp = x_prompt.reshape(1, batch * seq, D)
    xs = x_sample
    new_cache = (jnp.zeros((batch, depth, 2, h_b, seq, HEAD_DIM), F32),
                 jnp.zeros((batch, depth, h_b, seq, 2 * HEAD_DIM), F32),
                 jnp.zeros((batch, depth, h_c, seq, HEAD_DIM), F32),
                 jnp.zeros((batch, depth, h_c, seq, HEAD_DIM), F32))
    rowv = lambda a: a.reshape(depth, 1, -1)
    merge_w = (wmg_b, rowv(b_mgate), wa_b, wb_b, wc_b, wo_b, rowv(ln_g), rowv(ln_b))
    sg = (rowv(sg_norm_g), rowv(sg_norm_b), w_s_b, b_s_full)
    for l in range(depth):
        row2 = lambda a: a[l].reshape(1, -1)

        h, z, *new_cache = _inproj(xp, mod, l * mod_rows, w_in_b, blk, tn, new_cache=new_cache, layer=l)
        zc = z.reshape(batch, seq, d_in)
        yb, yc = _ctx_attn(zc, blk, d_b, d_c, lam_vec, row2(diff_subln_g), l, lam_inits[l])
        flat = lambda a: a.reshape(1, batch * seq, a.shape[-1])
        xp = _merge(xp, h, z, blk, flat(yb), flat(yc), mod, l * mod_rows, l, sg, *merge_w, alpha)

        h, z, vt = _inproj(xs, mod, l * mod_rows + 1, w_in_b, blk, tn, rope_tabs=rope_tabs, layer=l)
        yb = _diff_attn(z, vt, blk, d_b, lam_vec, row2(diff_subln_g), l, lam_inits[l],
                        cache=(cache_diff_k, cache_diff_v))
        yc = _na_attn(z, blk, d_c, cache_na_k, cache_na_v, bias_tab, l)
        xs = _merge(xs, h, z, blk, yb, yc, mod, l * mod_rows + 1, l, sg, *merge_w, alpha)

    return (xp.reshape(batch, seq, D), xs, *new_cache)
```
